```python
import jax, jax.numpy as jnp
from jax import lax
import numpy as np

D_MODEL = 1024
BATCH = 32
SEQ = 2048
DEPTH = 2

CHUNK = 64
N_MEM = 256
N_HEADS = 4
HEAD_DIM = 64
BRANCH_W = N_HEADS * HEAD_DIM
N_BRANCH = 4
Q_BLOCK = 128
ROPE_BASE = 10000.0
N_EXPERTS = 16
N_GROUPS = 4
EXPERTS_PER_GROUP = N_EXPERTS // N_GROUPS
TOP_K = 2
D_EXPERT = 512
DEEPNORM_ALPHA = (2.0 * DEPTH) ** 0.25
DEEPNORM_BETA = (8.0 * DEPTH) ** -0.25
LN_EPS = 1e-5
GN_EPS = 1e-6
MASK_VALUE = -1e30
FOX_FORGET_BIAS = 2.0

COL_SIZES = (
    BRANCH_W, BRANCH_W, BRANCH_W, N_HEADS,
    BRANCH_W, BRANCH_W, BRANCH_W, BRANCH_W,
    BRANCH_W, BRANCH_W, BRANCH_W, BRANCH_W,
    BRANCH_W,
    N_BRANCH * D_MODEL,
)
SPLIT_POINTS = tuple(int(v) for v in np.cumsum(COL_SIZES)[:-1])
N_IN = int(sum(COL_SIZES))
FOX_F_COL = 3 * BRANCH_W

kernel_name = 'hybrid_fox_retnet_hgrn2_mem_moe_deepnorm'


def _heads(t):
    return t.reshape(t.shape[:-1] + (N_HEADS, t.shape[-1] // N_HEADS))


def _layer_norm(x, g, b):
    xf = x.astype(jnp.float32)
    mu = jnp.mean(xf, axis=-1, keepdims=True)
    var = jnp.mean(jnp.square(xf - mu), axis=-1, keepdims=True)
    y = (xf - mu) * lax.rsqrt(var + LN_EPS)
    return (y * g + b).astype(x.dtype)


def _rope(t, pos):
    half = HEAD_DIM // 2
    inv = ROPE_BASE ** (-2.0 * jnp.arange(half, dtype=jnp.float32) / HEAD_DIM)
    ang = pos[:, None] * inv[None, :]
    cos = jnp.cos(ang)[:, None, :]
    sin = jnp.sin(ang)[:, None, :]
    t1, t2 = t[..., :half], t[..., half:]
    return jnp.concatenate([t1 * cos - t2 * sin, t1 * sin + t2 * cos], axis=-1)


def forgetting_attention(q, k, v, zf):
    S = q.shape[1]
    q = q.astype(jnp.float32)
    k = k.astype(jnp.float32)
    v = v.astype(jnp.float32)
    c = jnp.cumsum(jax.nn.log_sigmoid(zf.astype(jnp.float32)), axis=1).transpose(0, 2, 1)
    scale = HEAD_DIM ** -0.5
    pos = jnp.arange(S)
    outs = []
    for qs in range(0, S, Q_BLOCK):
        qe = qs + Q_BLOCK
        s = jnp.einsum('bqhd,bkhd->bhqk', q[:, qs:qe], k[:, :qe]) * scale
        s = s + c[:, :, qs:qe, None] - c[:, :, None, :qe]
        mask = pos[qs:qe, None] >= pos[None, :qe]
        p = jax.nn.softmax(jnp.where(mask, s, MASK_VALUE), axis=-1)
        outs.append(jnp.einsum('bhqk,bkhd->bqhd', p, v[:, :qe]))
    return jnp.concatenate(outs, axis=1)


def retention(q, k, v, g):
    B, S = q.shape[:2]
    n_chunks = S // CHUNK
    pos = jnp.arange(S, dtype=jnp.float32)
    q = _rope(q.astype(jnp.float32), pos)
    k = _rope(k.astype(jnp.float32), pos) * (HEAD_DIM ** -0.5)
    v = v.astype(jnp.float32)
    log_gamma = jnp.log1p(-jnp.exp2(-5.0 - jnp.arange(N_HEADS, dtype=jnp.float32)))
    idx = jnp.arange(CHUNK, dtype=jnp.float32)
    intra_decay = jnp.exp(log_gamma[:, None, None] * jnp.abs(idx[:, None] - idx[None, :]))
    key_decay = jnp.exp(log_gamma[:, None] * (CHUNK - 1.0 - idx[None, :]))
    query_decay = jnp.exp(log_gamma[:, None] * (idx[None, :] + 1.0))
    chunk_decay = jnp.exp(log_gamma * CHUNK)
    qc = q.reshape(B, n_chunks, CHUNK, N_HEADS, HEAD_DIM)
    kc = k.reshape(B, n_chunks, CHUNK, N_HEADS, HEAD_DIM)
    vc = v.reshape(B, n_chunks, CHUNK, N_HEADS, HEAD_DIM)
    scores = jnp.einsum('bnihd,bnjhd->bnhij', qc, kc) * intra_decay
    o = jnp.einsum('bnhij,bnjhv->bnihv', scores, vc)
    kv = jnp.einsum('bnjhd,hj,bnjhv->nbhdv', kc, key_decay, vc)

    def step(state, kv_n):
        return chunk_decay[None, :, None, None] * state + kv_n, state

    _, states = lax.scan(step, jnp.zeros((B, N_HEADS, HEAD_DIM, HEAD_DIM), jnp.float32), kv)
    o = o + jnp.einsum('bnihd,hi,nbhdv->bnihv', qc, query_decay, states)
    o = o.reshape(B, S, N_HEADS, HEAD_DIM)
    mu = jnp.mean(o, axis=-1, keepdims=True)
    var = jnp.mean(jnp.square(o - mu), axis=-1, keepdims=True)
    o = (o - mu) * lax.rsqrt(var + GN_EPS)
    return o * jax.nn.silu(g.astype(jnp.float32))


def _to_chunks(t):
    B, S, H, K = t.shape
    return t.reshape(B, S // CHUNK, CHUNK, H, K).transpose(1, 0, 3, 2, 4)


def hgrn2(q, i, zf, g, lb, norm_g):
    B, S = q.shape[:2]
    lb = lb.reshape(N_HEADS, HEAD_DIM)
    kf = (1.0 - lb) * jax.nn.sigmoid(-zf.astype(jnp.float32))
    logf = jnp.log1p(-kf)
    xs = (_to_chunks(q.astype(jnp.float32)), _to_chunks(kf), _to_chunks(i.astype(jnp.float32)), _to_chunks(logf))
    causal = jnp.tril(jnp.ones((CHUNK, CHUNK), dtype=bool))[:, :, None]

    def step(state, chunk):
        qn, kn, vn, lfn = chunk
        b = jnp.cumsum(lfn, axis=2)
        rel = b[:, :, :, None, :] - b[:, :, None, :, :]
        decay = jnp.where(causal, jnp.exp(jnp.where(causal, rel, 0.0)), 0.0)
        a = jnp.einsum('bhik,bhjk,bhijk->bhij', qn, kn, decay)
        o = jnp.einsum('bhij,bhjv->bhiv', a, vn) + jnp.einsum('bhik,bhkv->bhiv', qn * jnp.exp(b), state)
        b_last = b[:, :, -1:, :]
        new_state = jnp.exp(b_last[:, :, 0, :])[..., None] * state + jnp.einsum('bhjk,bhjv->bhkv', kn * jnp.exp(b_last - b), vn)
        return new_state, o

    _, o = lax.scan(step, jnp.zeros((B, N_HEADS, HEAD_DIM, HEAD_DIM), jnp.float32), xs)
    o = o.transpose(1, 0, 3, 2, 4).reshape(B, S, N_HEADS, HEAD_DIM)
    o = o * lax.rsqrt(jnp.mean(jnp.square(o), axis=-1, keepdims=True) + GN_EPS) * norm_g.astype(jnp.float32)
    return o * jax.nn.silu(g.astype(jnp.float32))


def memory_attention(q, mk, mv):
    s = jnp.einsum('bshd,bmhd->bhsm', q.astype(jnp.float32), mk.astype(jnp.float32)) * (HEAD_DIM ** -0.5)
    p = jax.nn.softmax(s, axis=-1)
    return jnp.einsum('bhsm,bmhd->bshd', p, mv.astype(jnp.float32))


def grouped_moe(x, w_router, b_router, w_e_in, w_e_out):
    B, S, D = x.shape
    T = B * S
    xt = x.reshape(T, D)
    probs = jax.nn.softmax((xt @ w_router + b_router).astype(jnp.float32), axis=-1)
    grouped = probs.reshape(T, N_GROUPS, EXPERTS_PER_GROUP)
    g_sel = jnp.argmax(jnp.max(grouped, axis=-1), axis=-1)
    in_group = jnp.take_along_axis(grouped, g_sel[:, None, None], axis=1)[:, 0]
    top_p, top_i = lax.top_k(in_group, TOP_K)
    expert = g_sel[:, None] * EXPERTS_PER_GROUP + top_i
    weight = top_p / jnp.sum(top_p, axis=-1, keepdims=True)
    flat_e = expert.reshape(-1)
    order = jnp.argsort(flat_e)
    tok = order // TOP_K
    xs = xt[tok]
    group_sizes = jnp.bincount(flat_e, length=N_EXPERTS).astype(jnp.int32)
    h = lax.ragged_dot(xs, w_e_in, group_sizes)
    h_gate, h_up = jnp.split(h, 2, axis=-1)
    y = lax.ragged_dot(jax.nn.silu(h_gate) * h_up, w_e_out, group_sizes)
    y = y * weight.reshape(-1)[order][:, None].astype(y.dtype)
    out = jnp.zeros_like(xt).at[tok].add(y.astype(xt.dtype))
    return out.reshape(B, S, D)


def setup_inputs(seed: int = 0) -> dict:
    key = jax.random.key(seed)
    ks = jax.random.split(key, 16)
    nrm = jax.random.normal
    f32 = jnp.float32
    x = nrm(ks[0], (BATCH, SEQ, D_MODEL), f32)
    mem = nrm(ks[1], (BATCH, N_MEM, D_MODEL), f32)
    w_in = nrm(ks[2], (DEPTH, D_MODEL, N_IN), f32) * D_MODEL ** -0.5
    b_in = 0.02 * nrm(ks[3], (DEPTH, N_IN), f32)
    b_in = b_in.at[:, FOX_F_COL:FOX_F_COL + N_HEADS].add(FOX_FORGET_BIAS)
    w_mem_kv = nrm(ks[4], (DEPTH, D_MODEL, 2 * BRANCH_W), f32) * D_MODEL ** -0.5
    hgrn_lb = 1.0 + 0.1 * nrm(ks[5], (DEPTH, BRANCH_W), f32)
    hgrn_norm_g = 1.0 + 0.1 * nrm(ks[6], (DEPTH, HEAD_DIM), f32)
    w_branch = nrm(ks[7], (DEPTH, N_BRANCH, BRANCH_W, D_MODEL), f32) * (BRANCH_W ** -0.5) * DEEPNORM_BETA
    w_out = nrm(ks[8], (DEPTH, D_MODEL, D_MODEL), f32) * (D_MODEL ** -0.5) * DEEPNORM_BETA
    ln_g = 1.0 + 0.1 * nrm(ks[9], (DEPTH, 2, D_MODEL), f32)
    ln_b = 0.02 * nrm(ks[10], (DEPTH, 2, D_MODEL), f32)
    w_router = nrm(ks[11], (D_MODEL, N_EXPERTS), f32) * D_MODEL ** -0.5
    b_router = 0.01 * nrm(ks[12], (N_EXPERTS,), f32)
    w_e_in = nrm(ks[13], (DEPTH, N_EXPERTS, D_MODEL, 2 * D_EXPERT), f32) * D_MODEL ** -0.5
    w_e_out = nrm(ks[14], (DEPTH, N_EXPERTS, D_EXPERT, D_MODEL), f32) * (D_EXPERT ** -0.5) * DEEPNORM_BETA
    return {'x': x, 'mem': mem, 'w_in': w_in, 'b_in': b_in, 'w_mem_kv': w_mem_kv,
            'hgrn_lb': hgrn_lb, 'hgrn_norm_g': hgrn_norm_g, 'w_branch': w_branch, 'w_out': w_out,
            'ln_g': ln_g, 'ln_b': ln_b, 'w_router': w_router, 'b_router': b_router,
            'w_e_in': w_e_in, 'w_e_out': w_e_out}


def reference(x, mem, w_in, b_in, w_mem_kv, hgrn_lb, hgrn_norm_g, w_branch, w_out,
              ln_g, ln_b, w_router, b_router, w_e_in, w_e_out):
    B, S, D = x.shape
    lb_all = jax.nn.softmax(hgrn_lb.astype(jnp.float32), axis=0)
    lb_all = jnp.cumsum(lb_all, axis=0) - lb_all[0:1]
    for l in range(DEPTH):
        proj = x @ w_in[l] + b_in[l]
        (fq, fk, fv, ff, rq, rk, rv, rg, hf, hi, hq, hg, mq, gate) = jnp.split(proj, SPLIT_POINTS, axis=-1)
        o_fox = forgetting_attention(_heads(fq), _heads(fk), _heads(fv), ff)
        o_ret = retention(_heads(rq), _heads(rk), _heads(rv), _heads(rg))
        o_hg = hgrn2(_heads(hq), _heads(hi), _heads(hf), _heads(hg), lb_all[l], hgrn_norm_g[l])
        mk, mv = jnp.split(mem @ w_mem_kv[l], 2, axis=-1)
        o_mem = memory_attention(_heads(mq), _heads(mk), _heads(mv))
        branches = jnp.stack([o_fox, o_ret, o_hg, o_mem], axis=2).reshape(B, S, N_BRANCH, BRANCH_W).astype(x.dtype)
        up = jnp.einsum('bsnc,ncd->bsnd', branches, w_branch[l])
        gates = jax.nn.sigmoid(gate.reshape(B, S, N_BRANCH, D))
        mixed = jnp.sum(gates * up, axis=2) @ w_out[l]
        x = _layer_norm(DEEPNORM_ALPHA * x + mixed, ln_g[l, 0], ln_b[l, 0])
        x = _layer_norm(DEEPNORM_ALPHA * x + grouped_moe(x, w_router, b_router, w_e_in[l], w_e_out[l]), ln_g[l, 1], ln_b[l, 1])
    return x
```

```python
import functools
import math

import jax
import jax.numpy as jnp
from jax import lax
from jax.experimental import pallas as pl
from jax.experimental.pallas import tpu as pltpu

F32 = jnp.float32
BF16 = jnp.bfloat16
I32 = jnp.int32
HIGHEST = lax.Precision.HIGHEST

N_HEADS = 4
HEAD_DIM = 64
BRANCH_W = N_HEADS * HEAD_DIM
N_BRANCH = 4
RET_CHUNK = 64
ROPE_BASE = 10000.0
N_EXPERTS = 16
EXPERTS_PER_GROUP = 4
LN_EPS = 1e-5
GN_EPS = 1e-6
MASK_VALUE = -1e30

LANES = 128
V7X_VMEM_LIMIT_BYTES = 52 * 1024 * 1024

PROJ_ROWS = 512
FOX_BLOCK = 512
MEM_ROWS = 512
RET_ROWS = 256
HGRN_ROWS = 128
HGRN_SUB = 8
MERGE_ROWS = 512
ROUTE_ROWS = 512
POS_LANES = 2048
DISPATCH_ROWS = 512
EXPERT_ROWS = 512
COMBINE_ROWS = 256

_W_FOX = 3 * BRANCH_W
_C_FOXF = _W_FOX
_C_RET = _C_FOXF + LANES
_C_HGF = _C_RET + 4 * BRANCH_W
_C_HGR = _C_HGF + BRANCH_W
_C_MQ = _C_HGR + 3 * BRANCH_W
_C_END = _C_MQ + BRANCH_W

_NT = (((1,), (1,)), ((), ()))


def _cparams(sem):
    return pltpu.CompilerParams(dimension_semantics=sem, vmem_limit_bytes=V7X_VMEM_LIMIT_BYTES)


def _head_id(shape):
    return lax.broadcasted_iota(I32, shape, len(shape) - 1) // HEAD_DIM


def _dot(a, b):
    return jnp.dot(a, b, preferred_element_type=F32)


def _dot_nt(a, b):
    return lax.dot_general(a, b, _NT, preferred_element_type=F32)


def _silu(x):
    return x * jax.nn.sigmoid(x)


def _layer_norm(z, g, b):
    mu = jnp.mean(z, axis=-1, keepdims=True)
    d = z - mu
    var = jnp.mean(d * d, axis=-1, keepdims=True)
    return d * lax.rsqrt(var + LN_EPS) * g + b


def _proj_kernel(x_ref, w_ref, b_ref, qkv_ref, ff_ref, ret_ref, hgf_ref, hgr_ref, mq_ref):
    xb = x_ref[...].astype(BF16)

    def seg(lo, hi):
        return _dot(xb, w_ref[:, lo:hi]) + b_ref[:, lo:hi]

    qkv_ref[...] = seg(0, _C_FOXF).astype(BF16)
    ff_ref[...] = seg(_C_FOXF, _C_RET)
    ret_ref[...] = seg(_C_RET, _C_HGF).astype(BF16)
    hgf_ref[...] = seg(_C_HGF, _C_HGR)
    hgr_ref[...] = seg(_C_HGR, _C_MQ).astype(BF16)
    mq_ref[...] = seg(_C_MQ, _C_END).astype(BF16)


def _proj(x2, w_mix, b_mix, tm):
    T, D = x2.shape
    widths = ((_C_FOXF, BF16), (LANES, F32), (4 * BRANCH_W, BF16), (BRANCH_W, F32),
              (3 * BRANCH_W, BF16), (BRANCH_W, BF16))
    return pl.pallas_call(
        _proj_kernel,
        out_shape=[jax.ShapeDtypeStruct((T, w), dt) for w, dt in widths],
        grid=(T // tm,),
        in_specs=[pl.BlockSpec((tm, D), lambda i: (i, 0)),
                  pl.BlockSpec((D, _C_END), lambda i: (0, 0)),
                  pl.BlockSpec((1, _C_END), lambda i: (0, 0))],
        out_specs=[pl.BlockSpec((tm, w), lambda i: (i, 0)) for w, _ in widths],
        compiler_params=_cparams(("parallel",)),
        name="mixer_proj",
    )(x2, w_mix, b_mix)


def _foxc_kernel(f_ref, c_ref):
    S = f_ref.shape[1]
    ls_t = jax.nn.log_sigmoid(f_ref[0]).T[:8, :]
    r = lax.broadcasted_iota(I32, (LANES, LANES), 0)
    c = lax.broadcasted_iota(I32, (LANES, LANES), 1)
    upper = (r <= c).astype(F32)
    carry = jnp.zeros((8, 1), F32)
    for blk in range(S // LANES):
        seg = ls_t[:, blk * LANES:(blk + 1) * LANES]
        cs = jnp.dot(seg, upper, precision=HIGHEST, preferred_element_type=F32) + carry
        c_ref[0, :, blk * LANES:(blk + 1) * LANES] = cs
        carry = cs[:, LANES - 1:LANES]


def _fox_cumlog(ff3):
    B, S, _ = ff3.shape
    return pl.pallas_call(
        _foxc_kernel,
        out_shape=jax.ShapeDtypeStruct((B, 8, S), F32),
        grid=(B,),
        in_specs=[pl.BlockSpec((1, S, LANES), lambda b: (b, 0, 0))],
        out_specs=pl.BlockSpec((1, 8, S), lambda b: (b, 0, 0)),
        compiler_params=_cparams(("parallel",)),
        name="fox_cumlog",
    )(ff3)


def _fox_kernel(q_ref, k_ref, v_ref, c_ref, o_ref, m_ref, l_ref, acc_ref):
    qi = pl.program_id(1)
    ki = pl.program_id(2)
    blk = q_ref.shape[1]

    @pl.when(ki == 0)
    def _():
        m_ref[...] = jnp.full(m_ref.shape, MASK_VALUE, F32)
        l_ref[...] = jnp.zeros(l_ref.shape, F32)
        acc_ref[...] = jnp.zeros(acc_ref.shape, F32)

    @pl.when(ki <= qi)
    def _():
        q = q_ref[0] * (HEAD_DIM ** -0.5)
        k = k_ref[0]
        v = v_ref[0]
        hid = _head_id((blk, BRANCH_W))
        rows = qi * blk + lax.broadcasted_iota(I32, (blk, blk), 0)
        cols = ki * blk + lax.broadcasted_iota(I32, (blk, blk), 1)
        causal = rows >= cols
        acc = acc_ref[...]
        for h in range(N_HEADS):
            qh = jnp.where(hid == h, q, jnp.zeros_like(q))
            s = _dot_nt(qh, k) - c_ref[0, h:h + 1, :]
            s = jnp.where(causal, s, MASK_VALUE)
            m_prev = m_ref[h]
            m_new = jnp.maximum(m_prev, jnp.max(s, axis=-1, keepdims=True))
            alpha = jnp.exp(m_prev - m_new)
            p = jnp.exp(s - m_new[:, :1])
            l_ref[h] = alpha * l_ref[h] + jnp.sum(p, axis=-1, keepdims=True)
            m_ref[h] = m_new
            pv = _dot(p.astype(BF16), v)
            acc = jnp.where(hid == h, acc * alpha[:, :1] + pv, acc)
        acc_ref[...] = acc

    @pl.when(ki == qi)
    def _():
        hid = _head_id((blk, BRANCH_W))
        acc = acc_ref[...]
        out = jnp.zeros_like(acc)
        for h in range(N_HEADS):
            out = jnp.where(hid == h, acc / l_ref[h][:, :1], out)
        o_ref[0] = out.astype(BF16)


def _fox_attention(qkv3, c, blk):
    B, S, _ = qkv3.shape
    n = S // blk
    return pl.pallas_call(
        _fox_kernel,
        out_shape=jax.ShapeDtypeStruct((B, S, BRANCH_W), BF16),
        grid=(B, n, n),
        in_specs=[pl.BlockSpec((1, blk, BRANCH_W), lambda b, qi, ki: (b, qi, 0)),
                  pl.BlockSpec((1, blk, BRANCH_W), lambda b, qi, ki: (b, jnp.minimum(ki, qi), 1)),
                  pl.BlockSpec((1, blk, BRANCH_W), lambda b, qi, ki: (b, jnp.minimum(ki, qi), 2)),
                  pl.BlockSpec((1, 8, blk), lambda b, qi, ki: (b, 0, jnp.minimum(ki, qi)))],
        out_specs=pl.BlockSpec((1, blk, BRANCH_W), lambda b, qi, ki: (b, qi, 0)),
        scratch_shapes=[pltpu.VMEM((N_HEADS, blk, LANES), F32),
                        pltpu.VMEM((N_HEADS, blk, LANES), F32),
                        pltpu.VMEM((blk, BRANCH_W), F32)],
        compiler_params=_cparams(("parallel", "parallel", "arbitrary")),
        name="fox_attention",
    )(qkv3, qkv3, qkv3, c)


def _mem_kernel(q_ref, mem_ref, w_ref, o_ref, kv_ref):
    @pl.when(pl.program_id(1) == 0)
    def _():
        kv_ref[...] = _dot(mem_ref[0].astype(BF16), w_ref[...]).astype(BF16)

    rows = q_ref.shape[1]
    q = q_ref[0] * (HEAD_DIM ** -0.5)
    mk = kv_ref[:, :BRANCH_W]
    mv = kv_ref[:, BRANCH_W:]
    hid = _head_id((rows, BRANCH_W))
    out = jnp.zeros((rows, BRANCH_W), F32)
    for h in range(N_HEADS):
        qh = jnp.where(hid == h, q, jnp.zeros_like(q))
        s = _dot_nt(qh, mk)
        p = jnp.exp(s - jnp.max(s, axis=-1, keepdims=True))
        pv = _dot(p.astype(BF16), mv)
        out = jnp.where(hid == h, pv / jnp.sum(p, axis=-1, keepdims=True), out)
    o_ref[0] = out.astype(BF16)


def _mem_attention(mq3, mem, w_kv, tm):
    B, S, _ = mq3.shape
    M, D = mem.shape[1:]
    return pl.pallas_call(
        _mem_kernel,
        out_shape=jax.ShapeDtypeStruct((B, S, BRANCH_W), BF16),
        grid=(B, S // tm),
        in_specs=[pl.BlockSpec((1, tm, BRANCH_W), lambda b, i: (b, i, 0)),
                  pl.BlockSpec((1, M, D), lambda b, i: (b, 0, 0)),
                  pl.BlockSpec((D, 2 * BRANCH_W), lambda b, i: (0, 0))],
        out_specs=pl.BlockSpec((1, tm, BRANCH_W), lambda b, i: (b, i, 0)),
        scratch_shapes=[pltpu.VMEM((M, 2 * BRANCH_W), BF16)],
        compiler_params=_cparams(("parallel", "arbitrary")),
        name="mem_attention",
    )(mq3, mem, w_kv)


def _block_diag(val_per_head):
    hid = jnp.arange(BRANCH_W) // HEAD_DIM
    same = hid[:, None] == hid[None, :]
    return jnp.where(same, jnp.asarray(val_per_head, F32)[hid][:, None], 0.0)


def _rope_tables(S):
    half = HEAD_DIM // 2
    inv = ROPE_BASE ** (-2.0 * jnp.arange(half, dtype=F32) / HEAD_DIM)
    ang = jnp.arange(S, dtype=F32)[:, None] * inv[None, :]
    cos = jnp.tile(jnp.cos(ang), (1, 2 * N_HEADS))
    sin = jnp.tile(jnp.sin(ang), (1, 2 * N_HEADS))
    d = jnp.arange(BRANCH_W)
    lo = (d % HEAD_DIM) < half
    perm = jnp.where(lo[None, :] & (d[:, None] == d[None, :] + half), -1.0, 0.0)
    perm = perm + jnp.where((~lo)[None, :] & (d[:, None] == d[None, :] - half), 1.0, 0.0)
    return cos, sin, perm.astype(BF16)


def _ret_tables(R):
    log_gamma = jnp.log1p(-jnp.exp2(-5.0 - jnp.arange(N_HEADS, dtype=F32)))
    idx = jnp.arange(R, dtype=F32)
    chunk = jnp.arange(R) // RET_CHUNK
    dist = jnp.abs(idx[:, None] - idx[None, :])
    visible = chunk[None, :] <= chunk[:, None]
    decay = jnp.where(visible[None], jnp.exp(log_gamma[:, None, None] * dist[None]), 0.0)
    lanes_lg = jnp.repeat(log_gamma, HEAD_DIM)[None, :]
    qdec = jnp.exp(lanes_lg * (idx[:, None] + 1.0))
    kdec = jnp.exp(lanes_lg * (R - 1.0 - idx[:, None]))
    sdec = _block_diag(jnp.exp(log_gamma * R))
    return decay, qdec, kdec, sdec


def _ret_kernel(q_ref, k_ref, v_ref, g_ref, cos_ref, sin_ref, perm_ref, decay_ref, qdec_ref, kdec_ref,
                sdec_ref, bd_ref, gmean_ref, o_ref, st_ref):
    @pl.when(pl.program_id(1) == 0)
    def _():
        st_ref[...] = jnp.zeros(st_ref.shape, F32)

    R = q_ref.shape[1]
    cos = cos_ref[...]
    sin = sin_ref[...]
    perm = perm_ref[...]
    qb = q_ref[0]
    kb = k_ref[0]
    v = v_ref[0]
    qr = qb.astype(F32) * cos + _dot(qb, perm) * sin
    kr = (kb.astype(F32) * cos + _dot(kb, perm) * sin) * (HEAD_DIM ** -0.5)
    qrb = qr.astype(BF16)
    krb = kr.astype(BF16)
    hid = _head_id((R, BRANCH_W))

    st = st_ref[...]
    o = _dot((qr * qdec_ref[...]).astype(BF16), st.astype(BF16))
    for h in range(N_HEADS):
        qh = jnp.where(hid == h, qrb, jnp.zeros_like(qrb))
        a = _dot_nt(qh, krb) * decay_ref[h]
        o = o + jnp.where(hid == h, _dot(a.astype(BF16), v), 0.0)
    kd_t = (kr * kdec_ref[...]).T.astype(BF16)
    st_ref[...] = st * sdec_ref[...] + _dot(kd_t, v) * bd_ref[...]

    gmean = gmean_ref[...]
    mu = _dot(o.astype(BF16), gmean)
    d = o - mu
    var = _dot((d * d).astype(BF16), gmean)
    on = d * lax.rsqrt(var + GN_EPS)
    o_ref[0] = (on * _silu(g_ref[0].astype(F32))).astype(BF16)


def _retention(ret3, R):
    B, S, _ = ret3.shape
    cos, sin, perm = _rope_tables(S)
    decay, qdec, kdec, sdec = _ret_tables(R)
    bd = _block_diag(jnp.ones((N_HEADS,), F32))
    gmean = (bd / HEAD_DIM).astype(BF16)
    col = lambda c: pl.BlockSpec((1, R, BRANCH_W), lambda b, j: (b, j, c))
    const2 = lambda a: pl.BlockSpec(a.shape, lambda b, j: (0, 0))
    return pl.pallas_call(
        _ret_kernel,
        out_shape=jax.ShapeDtypeStruct((B, S, BRANCH_W), BF16),
        grid=(B, S // R),
        in_specs=[col(0), col(1), col(2), col(3),
                  pl.BlockSpec((R, BRANCH_W), lambda b, j: (j, 0)),
                  pl.BlockSpec((R, BRANCH_W), lambda b, j: (j, 0)),
                  const2(perm),
                  pl.BlockSpec(decay.shape, lambda b, j: (0, 0, 0)),
                  const2(qdec), const2(kdec), const2(sdec), const2(bd), const2(gmean)],
        out_specs=pl.BlockSpec((1, R, BRANCH_W), lambda b, j: (b, j, 0)),
        scratch_shapes=[pltpu.VMEM((BRANCH_W, BRANCH_W), F32)],
        compiler_params=_cparams(("parallel", "arbitrary")),
        name="retention",
    )(ret3, ret3, ret3, ret3, cos, sin, perm, decay, qdec, kdec, sdec, bd, gmean)


def _hgrn_levels(R):
    levels = []
    m = HGRN_SUB
    while 2 * m <= R:
        levels.append(m)
        m *= 2
    return tuple(levels)


def _hgrn_kernel(f_ref, v_ref, q_ref, g_ref, lb_ref, ng_ref, tri_ref, seg_ref, bd_ref, gsum_ref, gmean_ref,
                 o_ref, st_ref):
    @pl.when(pl.program_id(1) == 0)
    def _():
        st_ref[...] = jnp.zeros(st_ref.shape, F32)

    R = f_ref.shape[1]
    W = BRANCH_W
    kf = (1.0 - lb_ref[...]) * jax.nn.sigmoid(-f_ref[0])
    logf = jnp.log1p(-kf)
    b = jnp.dot(tri_ref[...], logf, precision=HIGHEST, preferred_element_type=F32)
    q = q_ref[0].astype(F32)
    vb = v_ref[0]
    v = vb.astype(F32)
    hid = _head_id((R, W))
    row = lax.broadcasted_iota(I32, (R, W), 0)
    gsum = gsum_ref[...]

    o = _dot((q * kf).astype(BF16), gsum) * v
    sub = row % HGRN_SUB
    for d in range(1, HGRN_SUB):
        e = jnp.exp(jnp.minimum(b - pltpu.roll(b, d, 0), 0.0))
        w = jnp.where(sub >= d, q * pltpu.roll(kf, d, 0) * e, 0.0)
        o = o + _dot(w.astype(BF16), gsum) * pltpu.roll(v, d, 0)

    a_heads = [jnp.zeros((R, R), F32) for _ in range(N_HEADS)]
    for li, m in enumerate(_hgrn_levels(R)):
        nseg = R // (2 * m)
        bnd = jnp.broadcast_to(b.reshape(nseg, 2 * m, W)[:, m - 1:m, :], (nseg, 2 * m, W)).reshape(R, W)
        second = (row % (2 * m)) >= m
        qm = jnp.where(second, q * jnp.exp(jnp.minimum(b - bnd, 0.0)), 0.0).astype(BF16)
        km = jnp.where(second, 0.0, kf * jnp.exp(jnp.minimum(bnd - b, 0.0))).astype(BF16)
        for h in range(N_HEADS):
            qh = jnp.where(hid == h, qm, jnp.zeros_like(qm))
            a_heads[h] = a_heads[h] + _dot_nt(qh, km) * seg_ref[li]
    for h in range(N_HEADS):
        o = o + jnp.where(hid == h, _dot(a_heads[h].astype(BF16), vb), 0.0)

    st = st_ref[...]
    o = o + _dot_nt((q * jnp.exp(b)).astype(BF16), st.astype(BF16))
    b_last = b[R - 1:R, :]
    kd = (kf * jnp.exp(b_last - b)).astype(BF16)
    st_ref[...] = st * jnp.exp(b_last) + _dot(v.T.astype(BF16), kd) * bd_ref[...]

    ms = _dot((o * o).astype(BF16), gmean_ref[...])
    on = o * lax.rsqrt(ms + GN_EPS) * ng_ref[...]
    o_ref[0] = (on * _silu(g_ref[0].astype(F32))).astype(BF16)


def _hgrn2(hgf3, hgr3, lb_row, ng_row, R):
    B, S, _ = hgf3.shape
    idx = jnp.arange(R)
    tri = (idx[:, None] >= idx[None, :]).astype(F32)
    seg = jnp.stack([(idx[:, None] // (2 * m) == idx[None, :] // (2 * m)).astype(F32) for m in _hgrn_levels(R)])
    bd = _block_diag(jnp.ones((N_HEADS,), F32))
    gsum = bd.astype(BF16)
    gmean = (bd / HEAD_DIM).astype(BF16)
    col = lambda c: pl.BlockSpec((1, R, BRANCH_W), lambda b, j: (b, j, c))
    const2 = lambda a: pl.BlockSpec(a.shape, lambda b, j: (0, 0))
    return pl.pallas_call(
        _hgrn_kernel,
        out_shape=jax.ShapeDtypeStruct((B, S, BRANCH_W), BF16),
        grid=(B, S // R),
        in_specs=[col(0), col(0), col(1), col(2), const2(lb_row), const2(ng_row), const2(tri),
                  pl.BlockSpec(seg.shape, lambda b, j: (0, 0, 0)), const2(bd), const2(gsum), const2(gmean)],
        out_specs=pl.BlockSpec((1, R, BRANCH_W), lambda b, j: (b, j, 0)),
        scratch_shapes=[pltpu.VMEM((BRANCH_W, BRANCH_W), F32)],
        compiler_params=_cparams(("parallel", "arbitrary")),
        name="hgrn2",
    )(hgf3, hgr3, hgr3, hgr3, lb_row, ng_row, tri, seg, bd, gsum, gmean)


def _merge_kernel(x_ref, b0_ref, b1_ref, b2_ref, b3_ref, wg_ref, bg_ref, wb_ref, wo_ref, lng_ref, lnb_ref,
                  o_ref, *, alpha):
    x = x_ref[...]
    xb = x.astype(BF16)
    D = x.shape[1]
    mixed = jnp.zeros(x.shape, F32)
    for n, br in enumerate((b0_ref, b1_ref, b2_ref, b3_ref)):
        gate = jax.nn.sigmoid(_dot(xb, wg_ref[:, n * D:(n + 1) * D]) + bg_ref[:, n * D:(n + 1) * D])
        mixed = mixed + gate * _dot(br[...], wb_ref[n])
    z = alpha * x + _dot(mixed.astype(BF16), wo_ref[...])
    o_ref[...] = _layer_norm(z, lng_ref[...], lnb_ref[...])


def _merge(x2, branches, wg, bg, wb, wo, lng, lnb, alpha, tm):
    T, D = x2.shape
    const = lambda a: pl.BlockSpec(a.shape, lambda i: (0,) * a.ndim)
    br_spec = pl.BlockSpec((tm, BRANCH_W), lambda i: (i, 0))
    return pl.pallas_call(
        functools.partial(_merge_kernel, alpha=alpha),
        out_shape=jax.ShapeDtypeStruct((T, D), F32),
        grid=(T // tm,),
        in_specs=[pl.BlockSpec((tm, D), lambda i: (i, 0)), br_spec, br_spec, br_spec, br_spec,
                  const(wg), const(bg), const(wb), const(wo), const(lng), const(lnb)],
        out_specs=pl.BlockSpec((tm, D), lambda i: (i, 0)),
        compiler_params=_cparams(("parallel",)),
        name="merge_ln",
    )(x2, *branches, wg, bg, wb, wo, lng, lnb)


_R_W1, _R_W2, _R_E1, _R_E2, _R_RANK1, _R_RANK2 = range(6)


def _route_kernel(x_ref, wr_ref, br_ref, lstrict_ref, info_ref, rows_ref, cnt_ref, run_ref):
    @pl.when(pl.program_id(0) == 0)
    def _():
        run_ref[...] = jnp.zeros(run_ref.shape, F32)

    tm = x_ref.shape[0]
    logits = jnp.dot(x_ref[...], wr_ref[...], precision=HIGHEST, preferred_element_type=F32) + br_ref[...]
    lane = lax.broadcasted_iota(I32, (tm, LANES), 1).astype(F32)
    group = jnp.floor(lane * (1.0 / EXPERTS_PER_GROUP))
    e = jnp.exp(logits - jnp.max(logits, axis=-1, keepdims=True))
    p = e / jnp.sum(e, axis=-1, keepdims=True)
    p1 = jnp.max(p, axis=-1, keepdims=True)
    e1 = jnp.min(jnp.where(p == p1, lane, float(LANES)), axis=-1, keepdims=True)
    in_group = group == jnp.floor(e1 * (1.0 / EXPERTS_PER_GROUP))
    rest = jnp.where(in_group, jnp.where(lane == e1, -1.0, p), -1.0)
    p2 = jnp.max(rest, axis=-1, keepdims=True)
    e2 = jnp.min(jnp.where(rest == p2, lane, float(LANES)), axis=-1, keepdims=True)
    w1 = p1 / (p1 + p2)
    w2 = p2 / (p1 + p2)

    sel = jnp.where(lane == e1, 1.0, jnp.where(lane == e2, 1.0, 0.0))
    before = _dot(lstrict_ref[...], sel.astype(BF16)) + run_ref[0:1, :]
    rank1 = jnp.sum(jnp.where(lane == e1, before, 0.0), axis=-1, keepdims=True)
    rank2 = jnp.sum(jnp.where(lane == e2, before, 0.0), axis=-1, keepdims=True)
    run_ref[...] = run_ref[...] + jnp.sum(sel, axis=0, keepdims=True)
    cnt_ref[...] = run_ref[...]

    info = jnp.zeros((tm, LANES), F32)
    for col, val in ((_R_W1, w1), (_R_W2, w2), (_R_E1, e1), (_R_E2, e2), (_R_RANK1, rank1), (_R_RANK2, rank2)):
        info = jnp.where(lane == float(col), val, info)
    info_ref[...] = info
    pick = (lax.broadcasted_iota(I32, (8, LANES), 0) == lax.broadcasted_iota(I32, (8, LANES), 1)).astype(F32)
    rows_ref[...] = lax.dot_general(pick, info, _NT, precision=HIGHEST, preferred_element_type=F32)


def _route(x2, wr_pad, br_pad, tm):
    T, D = x2.shape
    idx = jnp.arange(tm)
    lstrict = (idx[:, None] > idx[None, :]).astype(BF16)
    return pl.pallas_call(
        _route_kernel,
        out_shape=[jax.ShapeDtypeStruct((T, LANES), F32), jax.ShapeDtypeStruct((8, T), F32),
                   jax.ShapeDtypeStruct((8, LANES), F32)],
        grid=(T // tm,),
        in_specs=[pl.BlockSpec((tm, D), lambda i: (i, 0)),
                  pl.BlockSpec((D, LANES), lambda i: (0, 0)),
                  pl.BlockSpec((1, LANES), lambda i: (0, 0)),
                  pl.BlockSpec((tm, tm), lambda i: (0, 0))],
        out_specs=[pl.BlockSpec((tm, LANES), lambda i: (i, 0)),
                   pl.BlockSpec((8, tm), lambda i: (0, i)),
                   pl.BlockSpec((8, LANES), lambda i: (0, 0))],
        scratch_shapes=[pltpu.VMEM((8, LANES), F32)],
        compiler_params=_cparams(("arbitrary",)),
        name="moe_route",
    )(x2, wr_pad, br_pad, lstrict)


def _pos_kernel(off_ref, rows_ref, pos_ref):
    rows = rows_ref[...]
    e = rows[_R_E1:_R_E2 + 1, :]
    start = jnp.zeros(e.shape, F32)
    for ex in range(N_EXPERTS):
        start = jnp.where(e == float(ex), off_ref[ex].astype(F32), start)
    pos = (rows[_R_RANK1:_R_RANK2 + 1, :] + start).astype(I32)
    pos_ref[...] = jnp.zeros(pos_ref.shape, I32)
    pos_ref[0:2, :] = pos


def _positions(offsets, rows, tl):
    T = rows.shape[1]
    return pl.pallas_call(
        _pos_kernel,
        out_shape=jax.ShapeDtypeStruct((8, T), I32),
        grid_spec=pltpu.PrefetchScalarGridSpec(
            num_scalar_prefetch=1, grid=(T // tl,),
            in_specs=[pl.BlockSpec((8, tl), lambda i, off: (0, i))],
            out_specs=pl.BlockSpec((8, tl), lambda i, off: (0, i))),
        compiler_params=_cparams(("parallel",)),
        name="moe_positions",
    )(offsets, rows)


def _dispatch_kernel(pos1_ref, pos2_ref, x_ref, xs_in_ref, xs_ref, sem):
    del xs_in_ref
    td = pos1_ref.shape[0]
    base = pl.program_id(0) * td

    def row_copy(t, pos_ref, s):
        return pltpu.make_async_copy(x_ref.at[pl.ds(base + t, 1)], xs_ref.at[pl.ds(pos_ref[t], 1)], sem.at[s])

    def issue(t, carry):
        row_copy(t, pos1_ref, 0).start()
        row_copy(t, pos2_ref, 1).start()
        return carry

    lax.fori_loop(0, td, issue, 0)
    for s in range(2):
        pltpu.make_async_copy(x_ref.at[pl.ds(0, td)], xs_ref.at[pl.ds(0, td)], sem.at[s]).wait()


def _dispatch(pos1, pos2, x2, n_slots, td):
    T, D = x2.shape
    xs0 = jnp.zeros((n_slots, D), F32)
    return pl.pallas_call(
        _dispatch_kernel,
        out_shape=jax.ShapeDtypeStruct((n_slots, D), F32),
        grid=(T // td,),
        in_specs=[pl.BlockSpec((td,), lambda i: (i,), memory_space=pltpu.SMEM),
                  pl.BlockSpec((td,), lambda i: (i,), memory_space=pltpu.SMEM),
                  pl.BlockSpec(memory_space=pl.ANY),
                  pl.BlockSpec(memory_space=pl.ANY)],
        out_specs=pl.BlockSpec(memory_space=pl.ANY),
        scratch_shapes=[pltpu.SemaphoreType.DMA((2,))],
        input_output_aliases={3: 0},
        compiler_params=_cparams(("arbitrary",)),
        name="moe_dispatch",
    )(pos1, pos2, x2, xs0)


def _expert_kernel(be_ref, nu_ref, xs_ref, wi_ref, wo_ref, ys_ref):
    del be_ref
    used = pl.program_id(0) < nu_ref[0]

    @pl.when(used)
    def _():
        F = wo_ref.shape[1]
        h = _dot(xs_ref[...].astype(BF16), wi_ref[0])
        act = _silu(h[:, :F]) * h[:, F:]
        ys_ref[...] = _dot(act.astype(BF16), wo_ref[0])

    @pl.when(jnp.logical_not(used))
    def _():
        ys_ref[...] = jnp.zeros(ys_ref.shape, F32)


def _experts(block_expert, n_used, xs, w_in, w_out, bm):
    n_slots, D = xs.shape
    E, _, F2 = w_in.shape
    row_blk = lambda i, be, nu: (jnp.minimum(i, nu[0] - 1), 0)
    return pl.pallas_call(
        _expert_kernel,
        out_shape=jax.ShapeDtypeStruct((n_slots, D), F32),
        grid_spec=pltpu.PrefetchScalarGridSpec(
            num_scalar_prefetch=2, grid=(n_slots // bm,),
            in_specs=[pl.BlockSpec((bm, D), row_blk),
                      pl.BlockSpec((1, D, F2), lambda i, be, nu: (be[i], 0, 0)),
                      pl.BlockSpec((1, F2 // 2, D), lambda i, be, nu: (be[i], 0, 0))],
            out_specs=pl.BlockSpec((bm, D), lambda i, be, nu: (i, 0))),
        compiler_params=_cparams(("arbitrary",)),
        name="moe_experts",
    )(block_expert, n_used, xs, w_in, w_out)


def _combine_kernel(pos1_ref, pos2_ref, info_ref, x_ref, ys_ref, lng_ref, lnb_ref, o_ref, buf_ref, sem, *, alpha):
    tc = pos1_ref.shape[0]

    def row_copy(t, pos_ref, s):
        return pltpu.make_async_copy(ys_ref.at[pl.ds(pos_ref[t], 1)], buf_ref.at[s, pl.ds(t, 1)], sem.at[s])

    def issue(t, carry):
        row_copy(t, pos1_ref, 0).start()
        row_copy(t, pos2_ref, 1).start()
        return carry

    lax.fori_loop(0, tc, issue, 0)
    for s in range(2):
        pltpu.make_async_copy(ys_ref.at[pl.ds(0, tc)], buf_ref.at[s], sem.at[s]).wait()
    info = info_ref[...]
    moe = info[:, _R_W1:_R_W1 + 1] * buf_ref[0] + info[:, _R_W2:_R_W2 + 1] * buf_ref[1]
    o_ref[...] = _layer_norm(alpha * x_ref[...] + moe, lng_ref[...], lnb_ref[...])


def _combine(pos1, pos2, info, x2, ys, lng, lnb, alpha, tc):
    T, D = x2.shape
    return pl.pallas_call(
        functools.partial(_combine_kernel, alpha=alpha),
        out_shape=jax.ShapeDtypeStruct((T, D), F32),
        grid=(T // tc,),
        in_specs=[pl.BlockSpec((tc,), lambda i: (i,), memory_space=pltpu.SMEM),
                  pl.BlockSpec((tc,), lambda i: (i,), memory_space=pltpu.SMEM),
                  pl.BlockSpec((tc, LANES), lambda i: (i, 0)),
                  pl.BlockSpec((tc, D), lambda i: (i, 0)),
                  pl.BlockSpec(memory_space=pl.ANY),
                  pl.BlockSpec((1, D), lambda i: (0, 0)),
                  pl.BlockSpec((1, D), lambda i: (0, 0))],
        out_specs=pl.BlockSpec((tc, D), lambda i: (i, 0)),
        scratch_shapes=[pltpu.VMEM((2, tc, D), F32), pltpu.SemaphoreType.DMA((2,))],
        compiler_params=_cparams(("arbitrary",)),
        name="moe_combine",
    )(pos1, pos2, info, x2, ys, lng, lnb)


def _moe(x2, wr_pad, br_pad, w_e_in, w_e_out, lng, lnb, alpha, tiles):
    T, D = x2.shape
    bm = tiles["expert"]
    info, rows, counts = _route(x2, wr_pad, br_pad, tiles["route"])
    cnt = counts[0, :N_EXPERTS].astype(I32)
    padded = ((cnt + bm - 1) // bm) * bm
    ends = jnp.cumsum(padded)
    offsets = ends - padded
    n_blocks = (2 * T) // bm + N_EXPERTS
    n_used = (ends[-1] // bm).astype(I32)
    blk_start = jnp.arange(n_blocks, dtype=I32) * bm
    block_expert = jnp.sum((blk_start[:, None] >= ends[None, :]).astype(I32), axis=1)
    last_expert = jnp.sum((blk_start[n_used - 1] >= ends).astype(I32))
    block_expert = jnp.where(jnp.arange(n_blocks) < n_used, block_expert, last_expert).astype(I32)

    pos = _positions(offsets.astype(I32), rows, tiles["pos"])
    pos1, pos2 = pos[0], pos[1]
    xs = _dispatch(pos1, pos2, x2, n_blocks * bm, tiles["dispatch"])
    ys = _experts(block_expert, n_used.reshape(1), xs, w_e_in, w_e_out, bm)
    return _combine(pos1, pos2, info, x2, ys, lng, lnb, alpha, tiles["combine"])


def _tiles(B, S):
    T = B * S
    pick = lambda want, n: math.gcd(want, n)
    return dict(proj=pick(PROJ_ROWS, T), fox=pick(FOX_BLOCK, S), mem=pick(MEM_ROWS, S), ret=pick(RET_ROWS, S),
                hgrn=pick(HGRN_ROWS, S), merge=pick(MERGE_ROWS, T), route=pick(ROUTE_ROWS, T),
                pos=pick(POS_LANES, T), dispatch=pick(DISPATCH_ROWS, T), expert=pick(EXPERT_ROWS, T),
                combine=pick(COMBINE_ROWS, T))


def _pack_mixer_weights(w, b):
    c = 3 * BRANCH_W
    pad_w = jnp.zeros((w.shape[0], LANES - N_HEADS), w.dtype)
    pad_b = jnp.zeros((LANES - N_HEADS,), b.dtype)
    n_mix = c + N_HEADS + 9 * BRANCH_W
    w_mix = jnp.concatenate([w[:, :c], w[:, c:c + N_HEADS], pad_w, w[:, c + N_HEADS:n_mix]], axis=1)
    b_mix = jnp.concatenate([b[:c], b[c:c + N_HEADS], pad_b, b[c + N_HEADS:n_mix]])
    return w_mix.astype(BF16), b_mix[None, :], w[:, n_mix:].astype(BF16), b[None, n_mix:]


def kernel(x, mem, w_in, b_in, w_mem_kv, hgrn_lb, hgrn_norm_g, w_branch, w_out, ln_g, ln_b, w_router, b_router,
           w_e_in, w_e_out):
    B, S, D = x.shape
    T = B * S
    depth = w_in.shape[0]
    alpha = (2.0 * depth) ** 0.25
    tiles = _tiles(B, S)

    lb_all = jax.nn.softmax(hgrn_lb.astype(F32), axis=0)
    lb_all = jnp.cumsum(lb_all, axis=0) - lb_all[0:1]
    wr_pad = jnp.concatenate([w_router, jnp.zeros((D, LANES - N_EXPERTS), F32)], axis=1)
    br_pad = jnp.concatenate([b_router, jnp.full((LANES - N_EXPERTS,), MASK_VALUE, F32)])[None, :]

    x2 = x.reshape(T, D)
    for l in range(depth):
        w_mix, b_mix, w_gate, b_gate = _pack_mixer_weights(w_in[l], b_in[l])
        qkv, ff, ret, hgf, hgr, mq = _proj(x2, w_mix, b_mix, tiles["proj"])
        r3 = lambda a: a.reshape(B, S, a.shape[-1])
        c = _fox_cumlog(r3(ff))
        o_fox = _fox_attention(r3(qkv), c, tiles["fox"])
        o_ret = _retention(r3(ret), tiles["ret"])
        o_hg = _hgrn2(r3(hgf), r3(hgr), lb_all[l][None, :], jnp.tile(hgrn_norm_g[l], N_HEADS)[None, :],
                      tiles["hgrn"])
        o_mem = _mem_attention(r3(mq), mem, w_mem_kv[l].astype(BF16), tiles["mem"])
        branches = [o.reshape(T, BRANCH_W) for o in (o_fox, o_ret, o_hg, o_mem)]
        x2 = _merge(x2, branches, w_gate, b_gate, w_branch[l].astype(BF16), w_out[l].astype(BF16),
                    ln_g[l, 0][None, :], ln_b[l, 0][None, :], alpha, tiles["merge"])
        x2 = _moe(x2, wr_pad, br_pad, w_e_in[l].astype(BF16), w_e_out[l].astype(BF16),
                  ln_g[l, 1][None, :], ln_b[l, 1][None, :], alpha, tiles)
    return x2.reshape(B, S, D)
```

```python
import functools
import math

import jax
import jax.numpy as jnp
from jax import lax
from jax.experimental import pallas as pl
from jax.experimental.pallas import tpu as pltpu

F32 = jnp.float32
BF16 = jnp.bfloat16
I32 = jnp.int32
HIGHEST = lax.Precision.HIGHEST

N_HEADS = 4
HEAD_DIM = 64
BRANCH_W = N_HEADS * HEAD_DIM
N_BRANCH = 4
RET_CHUNK = 64
ROPE_BASE = 10000.0
N_EXPERTS = 16
EXPERTS_PER_GROUP = 4
LN_EPS = 1e-5
GN_EPS = 1e-6
MASK_VALUE = -1e30

LANES = 128
V7X_VMEM_LIMIT_BYTES = 52 * 1024 * 1024

PROJ_ROWS = 512
FOX_BLOCK = 512
MEM_ROWS = 512
RET_ROWS = 256
HGRN_ROWS = 128
HGRN_SUB = 8
MERGE_ROWS = 512
ROUTE_ROWS = 512
POS_LANES = 2048
DISPATCH_ROWS = 512
EXPERT_ROWS = 512
COMBINE_ROWS = 256

_W_FOX = 3 * BRANCH_W
_C_FOXF = _W_FOX
_C_RET = _C_FOXF + LANES
_C_HGF = _C_RET + 4 * BRANCH_W
_C_HGR = _C_HGF + BRANCH_W
_C_MQ = _C_HGR + 3 * BRANCH_W
_C_END = _C_MQ + BRANCH_W

_NT = (((1,), (1,)), ((), ()))


def _cparams(sem):
    return pltpu.CompilerParams(dimension_semantics=sem, vmem_limit_bytes=V7X_VMEM_LIMIT_BYTES)


def _head_id(shape):
    return lax.broadcasted_iota(I32, shape, len(shape) - 1) // HEAD_DIM


def _dot(a, b):
    return jnp.dot(a, b, preferred_element_type=F32)


def _dot_nt(a, b):
    return lax.dot_general(a, b, _NT, preferred_element_type=F32)


def _silu(x):
    return x * jax.nn.sigmoid(x)


def _layer_norm(z, g, b):
    mu = jnp.mean(z, axis=-1, keepdims=True)
    d = z - mu
    var = jnp.mean(d * d, axis=-1, keepdims=True)
    return d * lax.rsqrt(var + LN_EPS) * g + b


def _proj_kernel(x_ref, w_ref, b_ref, qkv_ref, ff_ref, ret_ref, hgf_ref, hgr_ref, mq_ref):
    xb = x_ref[...].astype(BF16)

    def seg(lo, hi):
        return _dot(xb, w_ref[:, lo:hi]) + b_ref[:, lo:hi]

    qkv_ref[...] = seg(0, _C_FOXF).astype(BF16)
    ff_ref[...] = seg(_C_FOXF, _C_RET)
    ret_ref[...] = seg(_C_RET, _C_HGF).astype(BF16)
    hgf_ref[...] = seg(_C_HGF, _C_HGR)
    hgr_ref[...] = seg(_C_HGR, _C_MQ).astype(BF16)
    mq_ref[...] = seg(_C_MQ, _C_END).astype(BF16)


def _proj(x2, w_mix, b_mix, tm):
    T, D = x2.shape
    widths = ((_C_FOXF, BF16), (LANES, F32), (4 * BRANCH_W, BF16), (BRANCH_W, F32),
              (3 * BRANCH_W, BF16), (BRANCH_W, BF16))
    return pl.pallas_call(
        _proj_kernel,
        out_shape=[jax.ShapeDtypeStruct((T, w), dt) for w, dt in widths],
        grid=(T // tm,),
        in_specs=[pl.BlockSpec((tm, D), lambda i: (i, 0)),
                  pl.BlockSpec((D, _C_END), lambda i: (0, 0)),
                  pl.BlockSpec((1, _C_END), lambda i: (0, 0))],
        out_specs=[pl.BlockSpec((tm, w), lambda i: (i, 0)) for w, _ in widths],
        compiler_params=_cparams(("parallel",)),
        name="mixer_proj",
    )(x2, w_mix, b_mix)


def _foxc_kernel(f_ref, c_ref):
    S = f_ref.shape[1]
    ls_t = jax.nn.log_sigmoid(f_ref[0]).T[:8, :]
    r = lax.broadcasted_iota(I32, (LANES, LANES), 0)
    c = lax.broadcasted_iota(I32, (LANES, LANES), 1)
    upper = (r <= c).astype(F32)
    carry = jnp.zeros((8, 1), F32)
    for blk in range(S // LANES):
        seg = ls_t[:, blk * LANES:(blk + 1) * LANES]
        cs = jnp.dot(seg, upper, precision=HIGHEST, preferred_element_type=F32) + carry
        c_ref[0, :, blk * LANES:(blk + 1) * LANES] = cs
        carry = cs[:, LANES - 1:LANES]


def _fox_cumlog(ff3):
    B, S, _ = ff3.shape
    return pl.pallas_call(
        _foxc_kernel,
        out_shape=jax.ShapeDtypeStruct((B, 8, S), F32),
        grid=(B,),
        in_specs=[pl.BlockSpec((1, S, LANES), lambda b: (b, 0, 0))],
        out_specs=pl.BlockSpec((1, 8, S), lambda b: (b, 0, 0)),
        compiler_params=_cparams(("parallel",)),
        name="fox_cumlog",
    )(ff3)


def _fox_kernel(q_ref, k_ref, v_ref, c_ref, o_ref, m_ref, l_ref, acc_ref):
    qi = pl.program_id(1)
    ki = pl.program_id(2)
    blk = q_ref.shape[1]

    @pl.when(ki == 0)
    def _():
        m_ref[...] = jnp.full(m_ref.shape, MASK_VALUE, F32)
        l_ref[...] = jnp.zeros(l_ref.shape, F32)
        acc_ref[...] = jnp.zeros(acc_ref.shape, F32)

    @pl.when(ki <= qi)
    def _():
        q = q_ref[0] * (HEAD_DIM ** -0.5)
        k = k_ref[0]
        v = v_ref[0]
        hid = _head_id((blk, BRANCH_W))
        rows = qi * blk + lax.broadcasted_iota(I32, (blk, blk), 0)
        cols = ki * blk + lax.broadcasted_iota(I32, (blk, blk), 1)
        causal = rows >= cols
        acc = acc_ref[...]
        for h in range(N_HEADS):
            qh = jnp.where(hid == h, q, jnp.zeros_like(q))
            s = _dot_nt(qh, k) - c_ref[0, h:h + 1, :]
            s = jnp.where(causal, s, MASK_VALUE)
            m_prev = m_ref[h]
            m_new = jnp.maximum(m_prev, jnp.max(s, axis=-1, keepdims=True))
            alpha = jnp.exp(m_prev - m_new)
            p = jnp.exp(s - m_new[:, :1])
            l_ref[h] = alpha * l_ref[h] + jnp.sum(p, axis=-1, keepdims=True)
            m_ref[h] = m_new
            pv = _dot(p.astype(BF16), v)
            acc = jnp.where(hid == h, acc * alpha[:, :1] + pv, acc)
        acc_ref[...] = acc

    @pl.when(ki == qi)
    def _():
        hid = _head_id((blk, BRANCH_W))
        acc = acc_ref[...]
        out = jnp.zeros_like(acc)
        for h in range(N_HEADS):
            out = jnp.where(hid == h, acc / l_ref[h][:, :1], out)
        o_ref[0] = out.astype(BF16)


def _fox_attention(qkv3, c, blk):
    B, S, _ = qkv3.shape
    n = S // blk
    return pl.pallas_call(
        _fox_kernel,
        out_shape=jax.ShapeDtypeStruct((B, S, BRANCH_W), BF16),
        grid=(B, n, n),
        in_specs=[pl.BlockSpec((1, blk, BRANCH_W), lambda b, qi, ki: (b, qi, 0)),
                  pl.BlockSpec((1, blk, BRANCH_W), lambda b, qi, ki: (b, jnp.minimum(ki, qi), 1)),
                  pl.BlockSpec((1, blk, BRANCH_W), lambda b, qi, ki: (b, jnp.minimum(ki, qi), 2)),
                  pl.BlockSpec((1, 8, blk), lambda b, qi, ki: (b, 0, jnp.minimum(ki, qi)))],
        out_specs=pl.BlockSpec((1, blk, BRANCH_W), lambda b, qi, ki: (b, qi, 0)),
        scratch_shapes=[pltpu.VMEM((N_HEADS, blk, LANES), F32),
                        pltpu.VMEM((N_HEADS, blk, LANES), F32),
                        pltpu.VMEM((blk, BRANCH_W), F32)],
        compiler_params=_cparams(("parallel", "parallel", "arbitrary")),
        name="fox_attention",
    )(qkv3, qkv3, qkv3, c)


def _mem_kernel(q_ref, mem_ref, w_ref, o_ref, kv_ref):
    @pl.when(pl.program_id(1) == 0)
    def _():
        kv_ref[...] = _dot(mem_ref[0].astype(BF16), w_ref[...]).astype(BF16)

    rows = q_ref.shape[1]
    q = q_ref[0] * (HEAD_DIM ** -0.5)
    mk = kv_ref[:, :BRANCH_W]
    mv = kv_ref[:, BRANCH_W:]
    hid = _head_id((rows, BRANCH_W))
    out = jnp.zeros((rows, BRANCH_W), F32)
    for h in range(N_HEADS):
        qh = jnp.where(hid == h, q, jnp.zeros_like(q))
        s = _dot_nt(qh, mk)
        p = jnp.exp(s - jnp.max(s, axis=-1, keepdims=True))
        pv = _dot(p.astype(BF16), mv)
        out = jnp.where(hid == h, pv / jnp.sum(p, axis=-1, keepdims=True), out)
    o_ref[0] = out.astype(BF16)


def _mem_attention(mq3, mem, w_kv, tm):
    B, S, _ = mq3.shape
    M, D = mem.shape[1:]
    return pl.pallas_call(
        _mem_kernel,
        out_shape=jax.ShapeDtypeStruct((B, S, BRANCH_W), BF16),
        grid=(B, S // tm),
        in_specs=[pl.BlockSpec((1, tm, BRANCH_W), lambda b, i: (b, i, 0)),
                  pl.BlockSpec((1, M, D), lambda b, i: (b, 0, 0)),
                  pl.BlockSpec((D, 2 * BRANCH_W), lambda b, i: (0, 0))],
        out_specs=pl.BlockSpec((1, tm, BRANCH_W), lambda b, i: (b, i, 0)),
        scratch_shapes=[pltpu.VMEM((M, 2 * BRANCH_W), BF16)],
        compiler_params=_cparams(("parallel", "arbitrary")),
        name="mem_attention",
    )(mq3, mem, w_kv)


def _block_diag(val_per_head):
    hid = jnp.arange(BRANCH_W) // HEAD_DIM
    same = hid[:, None] == hid[None, :]
    return jnp.where(same, jnp.asarray(val_per_head, F32)[hid][:, None], 0.0)


def _rope_tables(S):
    half = HEAD_DIM // 2
    inv = ROPE_BASE ** (-2.0 * jnp.arange(half, dtype=F32) / HEAD_DIM)
    ang = jnp.arange(S, dtype=F32)[:, None] * inv[None, :]
    cos = jnp.tile(jnp.cos(ang), (1, 2 * N_HEADS))
    sin = jnp.tile(jnp.sin(ang), (1, 2 * N_HEADS))
    d = jnp.arange(BRANCH_W)
    lo = (d % HEAD_DIM) < half
    perm = jnp.where(lo[None, :] & (d[:, None] == d[None, :] + half), -1.0, 0.0)
    perm = perm + jnp.where((~lo)[None, :] & (d[:, None] == d[None, :] - half), 1.0, 0.0)
    return cos, sin, perm.astype(BF16)


def _ret_tables(R):
    log_gamma = jnp.log1p(-jnp.exp2(-5.0 - jnp.arange(N_HEADS, dtype=F32)))
    idx = jnp.arange(R, dtype=F32)
    chunk = jnp.arange(R) // RET_CHUNK
    dist = jnp.abs(idx[:, None] - idx[None, :])
    visible = chunk[None, :] <= chunk[:, None]
    decay = jnp.where(visible[None], jnp.exp(log_gamma[:, None, None] * dist[None]), 0.0)
    lanes_lg = jnp.repeat(log_gamma, HEAD_DIM)[None, :]
    qdec = jnp.exp(lanes_lg * (idx[:, None] + 1.0))
    kdec = jnp.exp(lanes_lg * (R - 1.0 - idx[:, None]))
    sdec = _block_diag(jnp.exp(log_gamma * R))
    return decay, qdec, kdec, sdec


def _ret_kernel(q_ref, k_ref, v_ref, g_ref, cos_ref, sin_ref, perm_ref, decay_ref, qdec_ref, kdec_ref,
                sdec_ref, bd_ref, gmean_ref, o_ref, st_ref):
    @pl.when(pl.program_id(1) == 0)
    def _():
        st_ref[...] = jnp.zeros(st_ref.shape, F32)

    R = q_ref.shape[1]
    cos = cos_ref[...]
    sin = sin_ref[...]
    perm = perm_ref[...]
    qb = q_ref[0]
    kb = k_ref[0]
    v = v_ref[0]
    qr = qb.astype(F32) * cos + _dot(qb, perm) * sin
    kr = (kb.astype(F32) * cos + _dot(kb, perm) * sin) * (HEAD_DIM ** -0.5)
    qrb = qr.astype(BF16)
    krb = kr.astype(BF16)
    hid = _head_id((R, BRANCH_W))

    st = st_ref[...]
    o = _dot((qr * qdec_ref[...]).astype(BF16), st.astype(BF16))
    for h in range(N_HEADS):
        qh = jnp.where(hid == h, qrb, jnp.zeros_like(qrb))
        a = _dot_nt(qh, krb) * decay_ref[h]
        o = o + jnp.where(hid == h, _dot(a.astype(BF16), v), 0.0)
    kd_t = (kr * kdec_ref[...]).T.astype(BF16)
    st_ref[...] = st * sdec_ref[...] + _dot(kd_t, v) * bd_ref[...]

    gmean = gmean_ref[...]
    mu = _dot(o.astype(BF16), gmean)
    d = o - mu
    var = _dot((d * d).astype(BF16), gmean)
    on = d * lax.rsqrt(var + GN_EPS)
    o_ref[0] = (on * _silu(g_ref[0].astype(F32))).astype(BF16)


def _retention(ret3, R):
    B, S, _ = ret3.shape
    cos, sin, perm = _rope_tables(S)
    decay, qdec, kdec, sdec = _ret_tables(R)
    bd = _block_diag(jnp.ones((N_HEADS,), F32))
    gmean = (bd / HEAD_DIM).astype(BF16)
    col = lambda c: pl.BlockSpec((1, R, BRANCH_W), lambda b, j: (b, j, c))
    const2 = lambda a: pl.BlockSpec(a.shape, lambda b, j: (0, 0))
    return pl.pallas_call(
        _ret_kernel,
        out_shape=jax.ShapeDtypeStruct((B, S, BRANCH_W), BF16),
        grid=(B, S // R),
        in_specs=[col(0), col(1), col(2), col(3),
                  pl.BlockSpec((R, BRANCH_W), lambda b, j: (j, 0)),
                  pl.BlockSpec((R, BRANCH_W), lambda b, j: (j, 0)),
                  const2(perm),
                  pl.BlockSpec(decay.shape, lambda b, j: (0, 0, 0)),
                  const2(qdec), const2(kdec), const2(sdec), const2(bd), const2(gmean)],
        out_specs=pl.BlockSpec((1, R, BRANCH_W), lambda b, j: (b, j, 0)),
        scratch_shapes=[pltpu.VMEM((BRANCH_W, BRANCH_W), F32)],
        compiler_params=_cparams(("parallel", "arbitrary")),
        name="retention",
    )(ret3, ret3, ret3, ret3, cos, sin, perm, decay, qdec, kdec, sdec, bd, gmean)


def _hgrn_levels(R):
    levels = []
    m = HGRN_SUB
    while 2 * m <= R:
        levels.append(m)
        m *= 2
    return tuple(levels)


def _hgrn_kernel(f_ref, v_ref, q_ref, g_ref, lb_ref, ng_ref, tri_ref, seg_ref, bd_ref, gsum_ref, gmean_ref,
                 o_ref, st_ref):
    @pl.when(pl.program_id(1) == 0)
    def _():
        st_ref[...] = jnp.zeros(st_ref.shape, F32)

    R = f_ref.shape[1]
    W = BRANCH_W
    kf = (1.0 - lb_ref[...]) * jax.nn.sigmoid(-f_ref[0])
    logf = jnp.log1p(-kf)
    b = jnp.dot(tri_ref[...], logf, precision=HIGHEST, preferred_element_type=F32)
    q = q_ref[0].astype(F32)
    vb = v_ref[0]
    v = vb.astype(F32)
    hid = _head_id((R, W))
    row = lax.broadcasted_iota(I32, (R, W), 0)
    gsum = gsum_ref[...]

    o = _dot((q * kf).astype(BF16), gsum) * v
    sub = row % HGRN_SUB
    for d in range(1, HGRN_SUB):
        e = jnp.exp(jnp.minimum(b - pltpu.roll(b, d, 0), 0.0))
        w = jnp.where(sub >= d, q * pltpu.roll(kf, d, 0) * e, 0.0)
        o = o + _dot(w.astype(BF16), gsum) * pltpu.roll(v, d, 0)

    a_heads = [jnp.zeros((R, R), F32) for _ in range(N_HEADS)]
    for li, m in enumerate(_hgrn_levels(R)):
        nseg = R // (2 * m)
        bnd = jnp.broadcast_to(b.reshape(nseg, 2 * m, W)[:, m - 1:m, :], (nseg, 2 * m, W)).reshape(R, W)
        second = (row % (2 * m)) >= m
        qm = jnp.where(second, q * jnp.exp(jnp.minimum(b - bnd, 0.0)), 0.0).astype(BF16)
        km = jnp.where(second, 0.0, kf * jnp.exp(jnp.minimum(bnd - b, 0.0))).astype(BF16)
        for h in range(N_HEADS):
            qh = jnp.where(hid == h, qm, jnp.zeros_like(qm))
            a_heads[h] = a_heads[h] + _dot_nt(qh, km) * seg_ref[li]
    for h in range(N_HEADS):
        o = o + jnp.where(hid == h, _dot(a_heads[h].astype(BF16), vb), 0.0)

    st = st_ref[...]
    o = o + _dot_nt((q * jnp.exp(b)).astype(BF16), st.astype(BF16))
    b_last = b[R - 1:R, :]
    kd = (kf * jnp.exp(b_last - b)).astype(BF16)
    st_ref[...] = st * jnp.exp(b_last) + _dot(v.T.astype(BF16), kd) * bd_ref[...]

    ms = _dot((o * o).astype(BF16), gmean_ref[...])
    on = o * lax.rsqrt(ms + GN_EPS) * ng_ref[...]
    o_ref[0] = (on * _silu(g_ref[0].astype(F32))).astype(BF16)


def _hgrn2(hgf3, hgr3, lb_row, ng_row, R):
    B, S, _ = hgf3.shape
    idx = jnp.arange(R)
    tri = (idx[:, None] >= idx[None, :]).astype(F32)
    seg = jnp.stack([(idx[:, None] // (2 * m) == idx[None, :] // (2 * m)).astype(F32) for m in _hgrn_levels(R)])
    bd = _block_diag(jnp.ones((N_HEADS,), F32))
    gsum = bd.astype(BF16)
    gmean = (bd / HEAD_DIM).astype(BF16)
    col = lambda c: pl.BlockSpec((1, R, BRANCH_W), lambda b, j: (b, j, c))
    const2 = lambda a: pl.BlockSpec(a.shape, lambda b, j: (0, 0))
    return pl.pallas_call(
        _hgrn_kernel,
        out_shape=jax.ShapeDtypeStruct((B, S, BRANCH_W), BF16),
        grid=(B, S // R),
        in_specs=[col(0), col(0), col(1), col(2), const2(lb_row), const2(ng_row), const2(tri),
                  pl.BlockSpec(seg.shape, lambda b, j: (0, 0, 0)), const2(bd), const2(gsum), const2(gmean)],
        out_specs=pl.BlockSpec((1, R, BRANCH_W), lambda b, j: (b, j, 0)),
        scratch_shapes=[pltpu.VMEM((BRANCH_W, BRANCH_W), F32)],
        compiler_params=_cparams(("parallel", "arbitrary")),
        name="hgrn2",
    )(hgf3, hgr3, hgr3, hgr3, lb_row, ng_row, tri, seg, bd, gsum, gmean)


def _merge_kernel(x_ref, b0_ref, b1_ref, b2_ref, b3_ref, wg_ref, bg_ref, wb_ref, wo_ref, lng_ref, lnb_ref,
                  o_ref, *, alpha):
    x = x_ref[...]
    xb = x.astype(BF16)
    D = x.shape[1]
    mixed = jnp.zeros(x.shape, F32)
    for n, br in enumerate((b0_ref, b1_ref, b2_ref, b3_ref)):
        gate = jax.nn.sigmoid(_dot(xb, wg_ref[:, n * D:(n + 1) * D]) + bg_ref[:, n * D:(n + 1) * D])
        mixed = mixed + gate * _dot(br[...], wb_ref[n])
    z = alpha * x + _dot(mixed.astype(BF16), wo_ref[...])
    o_ref[...] = _layer_norm(z, lng_ref[...], lnb_ref[...])


def _merge(x2, branches, wg, bg, wb, wo, lng, lnb, alpha, tm):
    T, D = x2.shape
    const = lambda a: pl.BlockSpec(a.shape, lambda i: (0,) * a.ndim)
    br_spec = pl.BlockSpec((tm, BRANCH_W), lambda i: (i, 0))
    return pl.pallas_call(
        functools.partial(_merge_kernel, alpha=alpha),
        out_shape=jax.ShapeDtypeStruct((T, D), F32),
        grid=(T // tm,),
        in_specs=[pl.BlockSpec((tm, D), lambda i: (i, 0)), br_spec, br_spec, br_spec, br_spec,
                  const(wg), const(bg), const(wb), const(wo), const(lng), const(lnb)],
        out_specs=pl.BlockSpec((tm, D), lambda i: (i, 0)),
        compiler_params=_cparams(("parallel",)),
        name="merge_ln",
    )(x2, *branches, wg, bg, wb, wo, lng, lnb)


_R_W1, _R_W2, _R_E1, _R_E2, _R_RANK1, _R_RANK2 = range(6)


def _route_kernel(x_ref, wr_ref, br_ref, lstrict_ref, info_ref, rows_ref, cnt_ref, run_ref):
    @pl.when(pl.program_id(0) == 0)
    def _():
        run_ref[...] = jnp.zeros(run_ref.shape, F32)

    tm = x_ref.shape[0]
    logits = jnp.dot(x_ref[...], wr_ref[...], precision=HIGHEST, preferred_element_type=F32) + br_ref[...]
    lane = lax.broadcasted_iota(I32, (tm, LANES), 1).astype(F32)
    group = jnp.floor(lane * (1.0 / EXPERTS_PER_GROUP))
    e = jnp.exp(logits - jnp.max(logits, axis=-1, keepdims=True))
    p = e / jnp.sum(e, axis=-1, keepdims=True)
    p1 = jnp.max(p, axis=-1, keepdims=True)
    e1 = jnp.min(jnp.where(p == p1, lane, float(LANES)), axis=-1, keepdims=True)
    in_group = group == jnp.floor(e1 * (1.0 / EXPERTS_PER_GROUP))
    rest = jnp.where(in_group, jnp.where(lane == e1, -1.0, p), -1.0)
    p2 = jnp.max(rest, axis=-1, keepdims=True)
    e2 = jnp.min(jnp.where(rest == p2, lane, float(LANES)), axis=-1, keepdims=True)
    w1 = p1 / (p1 + p2)
    w2 = p2 / (p1 + p2)

    sel = jnp.where(lane == e1, 1.0, jnp.where(lane == e2, 1.0, 0.0))
    before = _dot(lstrict_ref[...], sel.astype(BF16)) + run_ref[0:1, :]
    rank1 = jnp.sum(jnp.where(lane == e1, before, 0.0), axis=-1, keepdims=True)
    rank2 = jnp.sum(jnp.where(lane == e2, before, 0.0), axis=-1, keepdims=True)
    run_ref[...] = run_ref[...] + jnp.sum(sel, axis=0, keepdims=True)
    cnt_ref[...] = run_ref[...]

    info = jnp.zeros((tm, LANES), F32)
    for col, val in ((_R_W1, w1), (_R_W2, w2), (_R_E1, e1), (_R_E2, e2), (_R_RANK1, rank1), (_R_RANK2, rank2)):
        info = jnp.where(lane == float(col), val, info)
    info_ref[...] = info
    pick = (lax.broadcasted_iota(I32, (8, LANES), 0) == lax.broadcasted_iota(I32, (8, LANES), 1)).astype(F32)
    rows_ref[...] = lax.dot_general(pick, info, _NT, precision=HIGHEST, preferred_element_type=F32)


def _route(x2, wr_pad, br_pad, tm):
    T, D = x2.shape
    idx = jnp.arange(tm)
    lstrict = (idx[:, None] > idx[None, :]).astype(BF16)
    return pl.pallas_call(
        _route_kernel,
        out_shape=[jax.ShapeDtypeStruct((T, LANES), F32), jax.ShapeDtypeStruct((8, T), F32),
                   jax.ShapeDtypeStruct((8, LANES), F32)],
        grid=(T // tm,),
        in_specs=[pl.BlockSpec((tm, D), lambda i: (i, 0)),
                  pl.BlockSpec((D, LANES), lambda i: (0, 0)),
                  pl.BlockSpec((1, LANES), lambda i: (0, 0)),
                  pl.BlockSpec((tm, tm), lambda i: (0, 0))],
        out_specs=[pl.BlockSpec((tm, LANES), lambda i: (i, 0)),
                   pl.BlockSpec((8, tm), lambda i: (0, i)),
                   pl.BlockSpec((8, LANES), lambda i: (0, 0))],
        scratch_shapes=[pltpu.VMEM((8, LANES), F32)],
        compiler_params=_cparams(("arbitrary",)),
        name="moe_route",
    )(x2, wr_pad, br_pad, lstrict)


def _pos_kernel(off_ref, rows_ref, pos_ref):
    rows = rows_ref[...]
    e = rows[_R_E1:_R_E2 + 1, :]
    start = jnp.zeros(e.shape, F32)
    for ex in range(N_EXPERTS):
        start = jnp.where(e == float(ex), off_ref[ex].astype(F32), start)
    pos = (rows[_R_RANK1:_R_RANK2 + 1, :] + start).astype(I32)
    pos_ref[...] = jnp.zeros(pos_ref.shape, I32)
    pos_ref[0:2, :] = pos


def _positions(offsets, rows, tl):
    T = rows.shape[1]
    return pl.pallas_call(
        _pos_kernel,
        out_shape=jax.ShapeDtypeStruct((8, T), I32),
        grid_spec=pltpu.PrefetchScalarGridSpec(
            num_scalar_prefetch=1, grid=(T // tl,),
            in_specs=[pl.BlockSpec((8, tl), lambda i, off: (0, i))],
            out_specs=pl.BlockSpec((8, tl), lambda i, off: (0, i))),
        compiler_params=_cparams(("parallel",)),
        name="moe_positions",
    )(offsets, rows)


def _from_slab(ref, idx, rows, slab):
    return jnp.concatenate([ref[idx + (pl.ds(s, rows, stride=slab), slice(None))] for s in range(slab)], axis=1)


def _dispatch_kernel(pos1_ref, pos2_ref, x_ref, xs_in_ref, xs_ref, slab_ref, sem):
    del xs_in_ref
    td, D = x_ref.shape
    slab = D // LANES
    i = pl.program_id(0)
    slot = i % 2

    def wait_slot(s):
        for _ in range(2):
            pltpu.make_async_copy(slab_ref.at[s], xs_ref.at[pl.ds(0, td * slab)], sem.at[s]).wait()

    @pl.when(i >= 2)
    def _():
        wait_slot(slot)

    x = x_ref[...]
    for s in range(slab):
        slab_ref[slot, pl.ds(s, td, stride=slab), :] = x[:, s * LANES:(s + 1) * LANES]

    def issue(t, carry):
        src = slab_ref.at[slot, pl.ds(pl.multiple_of(t * slab, slab), slab)]
        for pos_ref in (pos1_ref, pos2_ref):
            dst = xs_ref.at[pl.ds(pl.multiple_of(pos_ref[t] * slab, slab), slab)]
            pltpu.make_async_copy(src, dst, sem.at[slot]).start()
        return carry

    lax.fori_loop(0, td, issue, 0)

    last = pl.num_programs(0) - 1

    @pl.when(i == last)
    def _():
        wait_slot(slot)

    @pl.when(jnp.logical_and(i == last, i >= 1))
    def _():
        wait_slot(1 - slot)


def _dispatch(pos1, pos2, x2, n_slots, td):
    T, D = x2.shape
    slab = D // LANES
    xs0 = jnp.zeros((n_slots * slab, LANES), F32)
    return pl.pallas_call(
        _dispatch_kernel,
        out_shape=jax.ShapeDtypeStruct((n_slots * slab, LANES), F32),
        grid=(T // td,),
        in_specs=[pl.BlockSpec((td,), lambda i: (i,), memory_space=pltpu.SMEM),
                  pl.BlockSpec((td,), lambda i: (i,), memory_space=pltpu.SMEM),
                  pl.BlockSpec((td, D), lambda i: (i, 0)),
                  pl.BlockSpec(memory_space=pl.ANY)],
        out_specs=pl.BlockSpec(memory_space=pl.ANY),
        scratch_shapes=[pltpu.VMEM((2, td * slab, LANES), F32), pltpu.SemaphoreType.DMA((2,))],
        input_output_aliases={3: 0},
        compiler_params=_cparams(("arbitrary",)),
        name="moe_dispatch",
    )(pos1, pos2, x2, xs0)


def _expert_kernel(be_ref, nu_ref, xs_ref, wi_ref, wo_ref, ys_ref):
    del be_ref
    used = pl.program_id(0) < nu_ref[0]

    @pl.when(used)
    def _():
        F = wo_ref.shape[1]
        slab = wo_ref.shape[2] // LANES
        bm = xs_ref.shape[0] // slab
        x = _from_slab(xs_ref, (), bm, slab)
        h = _dot(x.astype(BF16), wi_ref[0])
        act = _silu(h[:, :F]) * h[:, F:]
        y = _dot(act.astype(BF16), wo_ref[0])
        for s in range(slab):
            ys_ref[pl.ds(s, bm, stride=slab), :] = y[:, s * LANES:(s + 1) * LANES]

    @pl.when(jnp.logical_not(used))
    def _():
        ys_ref[...] = jnp.zeros(ys_ref.shape, F32)


def _experts(block_expert, n_used, xs, w_in, w_out, bm):
    E, D, F2 = w_in.shape
    slab = D // LANES
    n_blocks = xs.shape[0] // (bm * slab)
    row_blk = lambda i, be, nu: (jnp.minimum(i, nu[0] - 1), 0)
    return pl.pallas_call(
        _expert_kernel,
        out_shape=jax.ShapeDtypeStruct(xs.shape, F32),
        grid_spec=pltpu.PrefetchScalarGridSpec(
            num_scalar_prefetch=2, grid=(n_blocks,),
            in_specs=[pl.BlockSpec((bm * slab, LANES), row_blk),
                      pl.BlockSpec((1, D, F2), lambda i, be, nu: (be[i], 0, 0)),
                      pl.BlockSpec((1, F2 // 2, D), lambda i, be, nu: (be[i], 0, 0))],
            out_specs=pl.BlockSpec((bm * slab, LANES), lambda i, be, nu: (i, 0))),
        compiler_params=_cparams(("arbitrary",)),
        name="moe_experts",
    )(block_expert, n_used, xs, w_in, w_out)


def _combine_kernel(pos1_ref, pos2_ref, info_ref, x_ref, ys_ref, lng_ref, lnb_ref, o_ref, buf_ref, sem, *, alpha):
    tc, D = x_ref.shape
    slab = D // LANES

    def issue(t, carry):
        for k, pos_ref in enumerate((pos1_ref, pos2_ref)):
            src = ys_ref.at[pl.ds(pl.multiple_of(pos_ref[t] * slab, slab), slab)]
            dst = buf_ref.at[k, pl.ds(pl.multiple_of(t * slab, slab), slab)]
            pltpu.make_async_copy(src, dst, sem.at[k]).start()
        return carry

    lax.fori_loop(0, tc, issue, 0)
    for k in range(2):
        pltpu.make_async_copy(ys_ref.at[pl.ds(0, tc * slab)], buf_ref.at[k], sem.at[k]).wait()
    info = info_ref[...]
    moe = (info[:, _R_W1:_R_W1 + 1] * _from_slab(buf_ref, (0,), tc, slab)
           + info[:, _R_W2:_R_W2 + 1] * _from_slab(buf_ref, (1,), tc, slab))
    o_ref[...] = _layer_norm(alpha * x_ref[...] + moe, lng_ref[...], lnb_ref[...])


def _combine(pos1, pos2, info, x2, ys, lng, lnb, alpha, tc):
    T, D = x2.shape
    return pl.pallas_call(
        functools.partial(_combine_kernel, alpha=alpha),
        out_shape=jax.ShapeDtypeStruct((T, D), F32),
        grid=(T // tc,),
        in_specs=[pl.BlockSpec((tc,), lambda i: (i,), memory_space=pltpu.SMEM),
                  pl.BlockSpec((tc,), lambda i: (i,), memory_space=pltpu.SMEM),
                  pl.BlockSpec((tc, LANES), lambda i: (i, 0)),
                  pl.BlockSpec((tc, D), lambda i: (i, 0)),
                  pl.BlockSpec(memory_space=pl.ANY),
                  pl.BlockSpec((1, D), lambda i: (0, 0)),
                  pl.BlockSpec((1, D), lambda i: (0, 0))],
        out_specs=pl.BlockSpec((tc, D), lambda i: (i, 0)),
        scratch_shapes=[pltpu.VMEM((2, tc * (D // LANES), LANES), F32), pltpu.SemaphoreType.DMA((2,))],
        compiler_params=_cparams(("arbitrary",)),
        name="moe_combine",
    )(pos1, pos2, info, x2, ys, lng, lnb)


def _moe(x2, wr_pad, br_pad, w_e_in, w_e_out, lng, lnb, alpha, tiles):
    T, D = x2.shape
    bm = tiles["expert"]
    info, rows, counts = _route(x2, wr_pad, br_pad, tiles["route"])
    cnt = counts[0, :N_EXPERTS].astype(I32)
    padded = ((cnt + bm - 1) // bm) * bm
    ends = jnp.cumsum(padded)
    offsets = ends - padded
    n_blocks = (2 * T) // bm + N_EXPERTS
    n_used = (ends[-1] // bm).astype(I32)
    blk_start = jnp.arange(n_blocks, dtype=I32) * bm
    block_expert = jnp.sum((blk_start[:, None] >= ends[None, :]).astype(I32), axis=1)
    last_expert = jnp.sum((blk_start[n_used - 1] >= ends).astype(I32))
    block_expert = jnp.where(jnp.arange(n_blocks) < n_used, block_expert, last_expert).astype(I32)

    pos = _positions(offsets.astype(I32), rows, tiles["pos"])
    pos1, pos2 = pos[0], pos[1]
    xs = _dispatch(pos1, pos2, x2, n_blocks * bm, tiles["dispatch"])
    ys = _experts(block_expert, n_used.reshape(1), xs, w_e_in, w_e_out, bm)
    return _combine(pos1, pos2, info, x2, ys, lng, lnb, alpha, tiles["combine"])


def _tiles(B, S):
    T = B * S
    pick = lambda want, n: math.gcd(want, n)
    return dict(proj=pick(PROJ_ROWS, T), fox=pick(FOX_BLOCK, S), mem=pick(MEM_ROWS, S), ret=pick(RET_ROWS, S),
                hgrn=pick(HGRN_ROWS, S), merge=pick(MERGE_ROWS, T), route=pick(ROUTE_ROWS, T),
                pos=pick(POS_LANES, T), dispatch=pick(DISPATCH_ROWS, T), expert=pick(EXPERT_ROWS, T),
                combine=pick(COMBINE_ROWS, T))


def _pack_mixer_weights(w, b):
    c = 3 * BRANCH_W
    pad_w = jnp.zeros((w.shape[0], LANES - N_HEADS), w.dtype)
    pad_b = jnp.zeros((LANES - N_HEADS,), b.dtype)
    n_mix = c + N_HEADS + 9 * BRANCH_W
    w_mix = jnp.concatenate([w[:, :c], w[:, c:c + N_HEADS], pad_w, w[:, c + N_HEADS:n_mix]], axis=1)
    b_mix = jnp.concatenate([b[:c], b[c:c + N_HEADS], pad_b, b[c + N_HEADS:n_mix]])
    return w_mix.astype(BF16), b_mix[None, :], w[:, n_mix:].astype(BF16), b[None, n_mix:]


def kernel(x, mem, w_in, b_in, w_mem_kv, hgrn_lb, hgrn_norm_g, w_branch, w_out, ln_g, ln_b, w_router, b_router,
           w_e_in, w_e_out):
    B, S, D = x.shape
    T = B * S
    depth = w_in.shape[0]
    alpha = (2.0 * depth) ** 0.25
    tiles = _tiles(B, S)

    lb_all = jax.nn.softmax(hgrn_lb.astype(F32), axis=0)
    lb_all = jnp.cumsum(lb_all, axis=0) - lb_all[0:1]
    wr_pad = jnp.concatenate([w_router, jnp.zeros((D, LANES - N_EXPERTS), F32)], axis=1)
    br_pad = jnp.concatenate([b_router, jnp.full((LANES - N_EXPERTS,), MASK_VALUE, F32)])[None, :]

    x2 = x.reshape(T, D)
    for l in range(depth):
        w_mix, b_mix, w_gate, b_gate = _pack_mixer_weights(w_in[l], b_in[l])
        qkv, ff, ret, hgf, hgr, mq = _proj(x2, w_mix, b_mix, tiles["proj"])
        r3 = lambda a: a.reshape(B, S, a.shape[-1])
        c = _fox_cumlog(r3(ff))
        o_fox = _fox_attention(r3(qkv), c, tiles["fox"])
        o_ret = _retention(r3(ret), tiles["ret"])
        o_hg = _hgrn2(r3(hgf), r3(hgr), lb_all[l][None, :], jnp.tile(hgrn_norm_g[l], N_HEADS)[None, :],
                      tiles["hgrn"])
        o_mem = _mem_attention(r3(mq), mem, w_mem_kv[l].astype(BF16), tiles["mem"])
        branches = [o.reshape(T, BRANCH_W) for o in (o_fox, o_ret, o_hg, o_mem)]
        x2 = _merge(x2, branches, w_gate, b_gate, w_branch[l].astype(BF16), w_out[l].astype(BF16),
                    ln_g[l, 0][None, :], ln_b[l, 0][None, :], alpha, tiles["merge"])
        x2 = _moe(x2, wr_pad, br_pad, w_e_in[l].astype(BF16), w_e_out[l].astype(BF16),
                  ln_g[l, 1][None, :], ln_b[l, 1][None, :], alpha, tiles)
    return x2.reshape(B, S, D)
```

```python
import functools
import math

import jax
import jax.numpy as jnp
from jax import lax
from jax.experimental import pallas as pl
from jax.experimental.pallas import tpu as pltpu

F32 = jnp.float32
BF16 = jnp.bfloat16
I32 = jnp.int32
HIGHEST = lax.Precision.HIGHEST

N_HEADS = 4
HEAD_DIM = 64
BRANCH_W = N_HEADS * HEAD_DIM
N_BRANCH = 4
RET_CHUNK = 64
ROPE_BASE = 10000.0
N_EXPERTS = 16
EXPERTS_PER_GROUP = 4
LN_EPS = 1e-5
GN_EPS = 1e-6
MASK_VALUE = -1e30
LOG2E = math.log2(math.e)

LANES = 128
V7X_VMEM_LIMIT_BYTES = 52 * 1024 * 1024

PROJ_ROWS = 512
FOX_BIAS_ROWS = 256
FOX_BLOCK = 512
MEM_ROWS = 512
RET_ROWS = 256
HGRN_ROWS = 128
HGRN_SUB = 8
MERGE_ROWS = 512
POS_LANES = 2048
DISPATCH_ROWS = 512
EXPERT_ROWS = 512
COMBINE_ROWS = 256

_W_FOX = 3 * BRANCH_W
_C_FOXF = _W_FOX
_C_RET = _C_FOXF + LANES
_C_HGF = _C_RET + 4 * BRANCH_W
_C_HGR = _C_HGF + BRANCH_W
_C_MQ = _C_HGR + 3 * BRANCH_W
_C_END = _C_MQ + BRANCH_W

_NT = (((1,), (1,)), ((), ()))


def _cparams(sem):
    return pltpu.CompilerParams(dimension_semantics=sem, vmem_limit_bytes=V7X_VMEM_LIMIT_BYTES)


def _head_id(shape):
    return lax.broadcasted_iota(I32, shape, len(shape) - 1) // HEAD_DIM


def _dot(a, b):
    return jnp.dot(a, b, preferred_element_type=F32)


def _dot_nt(a, b):
    return lax.dot_general(a, b, _NT, preferred_element_type=F32)


def _silu(x):
    return x * jax.nn.sigmoid(x)


def _layer_norm(z, g, b):
    mu = jnp.mean(z, axis=-1, keepdims=True)
    d = z - mu
    var = jnp.mean(d * d, axis=-1, keepdims=True)
    return d * lax.rsqrt(var + LN_EPS) * g + b


def _proj_kernel(x_ref, wt_ref, b_ref, qkv_ref, ff_ref, ret_ref, hgf_ref, hgr_ref, mq_ref):
    xb = x_ref[...].astype(BF16)

    def seg(lo, hi):
        return _dot_nt(xb, wt_ref[lo:hi, :]) + b_ref[:, lo:hi]

    qkv_ref[...] = seg(0, _C_FOXF).astype(BF16)
    ff_ref[...] = seg(_C_FOXF, _C_RET)
    ret_ref[...] = seg(_C_RET, _C_HGF).astype(BF16)
    hgf_ref[...] = seg(_C_HGF, _C_HGR)
    hgr_ref[...] = seg(_C_HGR, _C_MQ).astype(BF16)
    mq_ref[...] = seg(_C_MQ, _C_END).astype(BF16)


def _proj(x2, w_mix, b_mix, tm):
    T, D = x2.shape
    widths = ((_C_FOXF, BF16), (LANES, F32), (4 * BRANCH_W, BF16), (BRANCH_W, F32),
              (3 * BRANCH_W, BF16), (BRANCH_W, BF16))
    return pl.pallas_call(
        _proj_kernel,
        out_shape=[jax.ShapeDtypeStruct((T, w), dt) for w, dt in widths],
        grid=(T // tm,),
        in_specs=[pl.BlockSpec((tm, D), lambda i: (i, 0)),
                  pl.BlockSpec((_C_END, D), lambda i: (0, 0)),
                  pl.BlockSpec((1, _C_END), lambda i: (0, 0))],
        out_specs=[pl.BlockSpec((tm, w), lambda i: (i, 0)) for w, _ in widths],
        compiler_params=_cparams(("parallel",)),
        name="mixer_proj",
    )(x2, w_mix, b_mix)


_FOX_BIAS_PARTS = 3


def _fox_spare_lane(h):
    return HEAD_DIM * ((h + 1) % N_HEADS)


def _foxc_kernel(f_ref, place_ref, kb_ref, *, rows):
    S = f_ref.shape[1]
    r = lax.broadcasted_iota(I32, (rows, rows), 0)
    c = lax.broadcasted_iota(I32, (rows, rows), 1)
    lower = (r >= c).astype(F32)
    carry = jnp.zeros((1, LANES), F32)
    for blk in range(S // rows):
        ls = jax.nn.log_sigmoid(f_ref[0, blk * rows:(blk + 1) * rows, :]) * LOG2E
        cs = jnp.dot(lower, ls, precision=HIGHEST, preferred_element_type=F32) + carry
        carry = cs[rows - 1:rows, :]
        rest = -cs
        kb = jnp.zeros((rows, BRANCH_W), F32)
        for part in range(_FOX_BIAS_PARTS):
            piece = rest.astype(BF16)
            rest = rest - piece.astype(F32)
            kb = kb + _dot(piece, place_ref[part])
        kb_ref[0, blk * rows:(blk + 1) * rows, :] = kb.astype(BF16)


def _fox_key_bias(ff3, rows):
    B, S, _ = ff3.shape
    lane = jnp.arange(BRANCH_W)
    head = jnp.arange(LANES)
    dest = jnp.where(head < N_HEADS, HEAD_DIM * ((head + 1) % N_HEADS), -BRANCH_W)
    place = jnp.stack([(lane[None, :] == dest[:, None] + part) for part in range(_FOX_BIAS_PARTS)]).astype(BF16)
    return pl.pallas_call(
        functools.partial(_foxc_kernel, rows=rows),
        out_shape=jax.ShapeDtypeStruct((B, S, BRANCH_W), BF16),
        grid=(B,),
        in_specs=[pl.BlockSpec((1, S, LANES), lambda b: (b, 0, 0)),
                  pl.BlockSpec(place.shape, lambda b: (0, 0, 0))],
        out_specs=pl.BlockSpec((1, S, BRANCH_W), lambda b: (b, 0, 0)),
        compiler_params=_cparams(("parallel",)),
        name="fox_key_bias",
    )(ff3, place)


def _fox_lane_table():
    lane = jnp.arange(BRANCH_W)
    rows = []
    for h in range(N_HEADS):
        spare = _fox_spare_lane(h)
        bias = (lane >= spare) & (lane < spare + _FOX_BIAS_PARTS)
        rows += [lane // HEAD_DIM == h, bias, ~bias, lane == spare]
    return jnp.stack(rows).astype(F32)


def _fox_kernel(q_ref, k_ref, v_ref, kb_ref, lanes_ref, o_ref, qh_ref, m_ref, acc_ref):
    qi = pl.program_id(1)
    ki = pl.program_id(2)
    blk = q_ref.shape[1]

    def pattern(h, r):
        return lanes_ref[4 * h + r:4 * h + r + 1, :].astype(BF16)

    @pl.when(ki == 0)
    def _():
        m_ref[...] = jnp.full(m_ref.shape, MASK_VALUE, F32)
        acc_ref[...] = jnp.zeros(acc_ref.shape, F32)
        q = q_ref[0]
        for h in range(N_HEADS):
            qh_ref[h] = q * pattern(h, 0) + pattern(h, 1)

    def sweep(masked):
        k = k_ref[0]
        v = v_ref[0]
        kb = kb_ref[0]
        if masked:
            causal = lax.broadcasted_iota(I32, (blk, blk), 0) >= lax.broadcasted_iota(I32, (blk, blk), 1)
        for h in range(N_HEADS):
            qh = qh_ref[h]
            kh = k * pattern(h, 2) + kb * pattern(h, 1)
            ones_lane = pattern(h, 3)
            vh = v * (1 - ones_lane) + ones_lane
            s = _dot_nt(qh, kh)
            if masked:
                s = jnp.where(causal, s, MASK_VALUE)
            m_prev = m_ref[h]
            m_new = jnp.maximum(m_prev, jnp.max(s, axis=-1, keepdims=True))
            alpha = jnp.exp2(m_prev - m_new)
            p = jnp.exp2(s - jnp.concatenate([m_new] * (blk // LANES), axis=1))
            m_ref[h] = m_new
            acc_ref[h] = acc_ref[h] * jnp.concatenate([alpha] * (BRANCH_W // LANES), axis=1) + _dot(p.astype(BF16), vh)

    @pl.when(ki < qi)
    def _():
        sweep(False)

    @pl.when(ki == qi)
    def _():
        sweep(True)
        lane = lax.broadcasted_iota(I32, (blk, BRANCH_W), 1)
        out = jnp.zeros((blk, BRANCH_W), F32)
        for h in range(N_HEADS):
            acc = acc_ref[h]
            spare = _fox_spare_lane(h)
            out = jnp.where((lane // HEAD_DIM) == h, acc / acc[:, spare:spare + 1], out)
        o_ref[0] = out.astype(BF16)


def _fox_attention(qkv3, kb, blk):
    B, S, _ = qkv3.shape
    n = S // blk
    kv_blk = lambda c: pl.BlockSpec((1, blk, BRANCH_W), lambda b, qi, ki: (b, jnp.minimum(ki, qi), c))
    lanes = _fox_lane_table()
    return pl.pallas_call(
        _fox_kernel,
        out_shape=jax.ShapeDtypeStruct((B, S, BRANCH_W), BF16),
        grid=(B, n, n),
        in_specs=[pl.BlockSpec((1, blk, BRANCH_W), lambda b, qi, ki: (b, qi, 0)), kv_blk(1), kv_blk(2),
                  pl.BlockSpec((1, blk, BRANCH_W), lambda b, qi, ki: (b, jnp.minimum(ki, qi), 0)),
                  pl.BlockSpec(lanes.shape, lambda b, qi, ki: (0, 0))],
        out_specs=pl.BlockSpec((1, blk, BRANCH_W), lambda b, qi, ki: (b, qi, 0)),
        scratch_shapes=[pltpu.VMEM((N_HEADS, blk, BRANCH_W), BF16),
                        pltpu.VMEM((N_HEADS, blk, LANES), F32),
                        pltpu.VMEM((N_HEADS, blk, BRANCH_W), F32)],
        compiler_params=_cparams(("parallel", "parallel", "arbitrary")),
        name="fox_attention",
    )(qkv3, qkv3, qkv3, kb, lanes)


def _mem_kernel(q_ref, mem_ref, w_ref, o_ref, kv_ref):
    @pl.when(pl.program_id(1) == 0)
    def _():
        kv_ref[...] = _dot(mem_ref[0].astype(BF16), w_ref[...]).astype(BF16)

    rows = q_ref.shape[1]
    q = q_ref[0] * (HEAD_DIM ** -0.5)
    mk = kv_ref[:, :BRANCH_W]
    mv = kv_ref[:, BRANCH_W:]
    hid = _head_id((rows, BRANCH_W))
    out = jnp.zeros((rows, BRANCH_W), F32)
    for h in range(N_HEADS):
        qh = jnp.where(hid == h, q, jnp.zeros_like(q))
        s = _dot_nt(qh, mk)
        p = jnp.exp(s - jnp.max(s, axis=-1, keepdims=True))
        pv = _dot(p.astype(BF16), mv)
        out = jnp.where(hid == h, pv / jnp.sum(p, axis=-1, keepdims=True), out)
    o_ref[0] = out.astype(BF16)


def _mem_attention(mq3, mem, w_kv, tm):
    B, S, _ = mq3.shape
    M, D = mem.shape[1:]
    return pl.pallas_call(
        _mem_kernel,
        out_shape=jax.ShapeDtypeStruct((B, S, BRANCH_W), BF16),
        grid=(B, S // tm),
        in_specs=[pl.BlockSpec((1, tm, BRANCH_W), lambda b, i: (b, i, 0)),
                  pl.BlockSpec((1, M, D), lambda b, i: (b, 0, 0)),
                  pl.BlockSpec((D, 2 * BRANCH_W), lambda b, i: (0, 0))],
        out_specs=pl.BlockSpec((1, tm, BRANCH_W), lambda b, i: (b, i, 0)),
        scratch_shapes=[pltpu.VMEM((M, 2 * BRANCH_W), BF16)],
        compiler_params=_cparams(("parallel", "arbitrary")),
        name="mem_attention",
    )(mq3, mem, w_kv)


def _block_diag(val_per_head):
    hid = jnp.arange(BRANCH_W) // HEAD_DIM
    same = hid[:, None] == hid[None, :]
    return jnp.where(same, jnp.asarray(val_per_head, F32)[hid][:, None], 0.0)


def _rope_tables(S):
    half = HEAD_DIM // 2
    inv = ROPE_BASE ** (-2.0 * jnp.arange(half, dtype=F32) / HEAD_DIM)
    ang = jnp.arange(S, dtype=F32)[:, None] * inv[None, :]
    cos = jnp.tile(jnp.cos(ang), (1, 2 * N_HEADS))
    sin = jnp.tile(jnp.sin(ang), (1, 2 * N_HEADS))
    d = jnp.arange(BRANCH_W)
    lo = (d % HEAD_DIM) < half
    perm = jnp.where(lo[None, :] & (d[:, None] == d[None, :] + half), -1.0, 0.0)
    perm = perm + jnp.where((~lo)[None, :] & (d[:, None] == d[None, :] - half), 1.0, 0.0)
    return cos, sin, perm.astype(BF16)


def _ret_tables(R):
    log_gamma = jnp.log1p(-jnp.exp2(-5.0 - jnp.arange(N_HEADS, dtype=F32)))
    idx = jnp.arange(R, dtype=F32)
    chunk = jnp.arange(R) // RET_CHUNK
    dist = jnp.abs(idx[:, None] - idx[None, :])
    visible = chunk[None, :] <= chunk[:, None]
    decay = jnp.where(visible[None], jnp.exp(log_gamma[:, None, None] * dist[None]), 0.0)
    lanes_lg = jnp.repeat(log_gamma, HEAD_DIM)[None, :]
    qdec = jnp.exp(lanes_lg * (idx[:, None] + 1.0))
    kdec = jnp.exp(lanes_lg * (R - 1.0 - idx[:, None]))
    sdec = _block_diag(jnp.exp(log_gamma * R))
    return decay, qdec, kdec, sdec


def _ret_kernel(q_ref, k_ref, v_ref, g_ref, cos_ref, sin_ref, perm_ref, decay_ref, qdec_ref, kdec_ref,
                sdec_ref, bd_ref, gmean_ref, o_ref, st_ref):
    @pl.when(pl.program_id(1) == 0)
    def _():
        st_ref[...] = jnp.zeros(st_ref.shape, F32)

    R = q_ref.shape[1]
    cos = cos_ref[...]
    sin = sin_ref[...]
    perm = perm_ref[...]
    qb = q_ref[0]
    kb = k_ref[0]
    v = v_ref[0]
    qr = qb.astype(F32) * cos + _dot(qb, perm) * sin
    kr = (kb.astype(F32) * cos + _dot(kb, perm) * sin) * (HEAD_DIM ** -0.5)
    qrb = qr.astype(BF16)
    krb = kr.astype(BF16)
    hid = _head_id((R, BRANCH_W))

    st = st_ref[...]
    o = _dot((qr * qdec_ref[...]).astype(BF16), st.astype(BF16))
    for h in range(N_HEADS):
        qh = jnp.where(hid == h, qrb, jnp.zeros_like(qrb))
        a = _dot_nt(qh, krb) * decay_ref[h]
        o = o + jnp.where(hid == h, _dot(a.astype(BF16), v), 0.0)
    kd_t = (kr * kdec_ref[...]).T.astype(BF16)
    st_ref[...] = st * sdec_ref[...] + _dot(kd_t, v) * bd_ref[...]

    gmean = gmean_ref[...]
    mu = _dot(o.astype(BF16), gmean)
    d = o - mu
    var = _dot((d * d).astype(BF16), gmean)
    on = d * lax.rsqrt(var + GN_EPS)
    o_ref[0] = (on * _silu(g_ref[0].astype(F32))).astype(BF16)


def _retention(ret3, R):
    B, S, _ = ret3.shape
    cos, sin, perm = _rope_tables(S)
    decay, qdec, kdec, sdec = _ret_tables(R)
    bd = _block_diag(jnp.ones((N_HEADS,), F32))
    gmean = (bd / HEAD_DIM).astype(BF16)
    col = lambda c: pl.BlockSpec((1, R, BRANCH_W), lambda b, j: (b, j, c))
    const2 = lambda a: pl.BlockSpec(a.shape, lambda b, j: (0, 0))
    return pl.pallas_call(
        _ret_kernel,
        out_shape=jax.ShapeDtypeStruct((B, S, BRANCH_W), BF16),
        grid=(B, S // R),
        in_specs=[col(0), col(1), col(2), col(3),
                  pl.BlockSpec((R, BRANCH_W), lambda b, j: (j, 0)),
                  pl.BlockSpec((R, BRANCH_W), lambda b, j: (j, 0)),
                  const2(perm),
                  pl.BlockSpec(decay.shape, lambda b, j: (0, 0, 0)),
                  const2(qdec), const2(kdec), const2(sdec), const2(bd), const2(gmean)],
        out_specs=pl.BlockSpec((1, R, BRANCH_W), lambda b, j: (b, j, 0)),
        scratch_shapes=[pltpu.VMEM((BRANCH_W, BRANCH_W), F32)],
        compiler_params=_cparams(("parallel", "arbitrary")),
        name="retention",
    )(ret3, ret3, ret3, ret3, cos, sin, perm, decay, qdec, kdec, sdec, bd, gmean)


def _hgrn_levels(R):
    levels = []
    m = HGRN_SUB
    while 2 * m <= R:
        levels.append(m)
        m *= 2
    return tuple(levels)


def _hgrn_kernel(f_ref, v_ref, q_ref, g_ref, lb_ref, ng_ref, tri_ref, seg_ref, bd_ref, gsum_ref, gmean_ref,
                 o_ref, st_ref):
    @pl.when(pl.program_id(1) == 0)
    def _():
        st_ref[...] = jnp.zeros(st_ref.shape, F32)

    R = f_ref.shape[1]
    W = BRANCH_W
    kf = (1.0 - lb_ref[...]) * jax.nn.sigmoid(-f_ref[0])
    logf = jnp.log1p(-kf)
    b = jnp.dot(tri_ref[...], logf, precision=HIGHEST, preferred_element_type=F32)
    q = q_ref[0].astype(F32)
    vb = v_ref[0]
    v = vb.astype(F32)
    hid = _head_id((R, W))
    row = lax.broadcasted_iota(I32, (R, W), 0)
    gsum = gsum_ref[...]

    o = _dot((q * kf).astype(BF16), gsum) * v
    sub = row % HGRN_SUB
    for d in range(1, HGRN_SUB):
        e = jnp.exp(jnp.minimum(b - pltpu.roll(b, d, 0), 0.0))
        w = jnp.where(sub >= d, q * pltpu.roll(kf, d, 0) * e, 0.0)
        o = o + _dot(w.astype(BF16), gsum) * pltpu.roll(v, d, 0)

    a_heads = [jnp.zeros((R, R), F32) for _ in range(N_HEADS)]
    for li, m in enumerate(_hgrn_levels(R)):
        nseg = R // (2 * m)
        bnd = jnp.broadcast_to(b.reshape(nseg, 2 * m, W)[:, m - 1:m, :], (nseg, 2 * m, W)).reshape(R, W)
        second = (row % (2 * m)) >= m
        qm = jnp.where(second, q * jnp.exp(jnp.minimum(b - bnd, 0.0)), 0.0).astype(BF16)
        km = jnp.where(second, 0.0, kf * jnp.exp(jnp.minimum(bnd - b, 0.0))).astype(BF16)
        for h in range(N_HEADS):
            qh = jnp.where(hid == h, qm, jnp.zeros_like(qm))
            a_heads[h] = a_heads[h] + _dot_nt(qh, km) * seg_ref[li]
    for h in range(N_HEADS):
        o = o + jnp.where(hid == h, _dot(a_heads[h].astype(BF16), vb), 0.0)

    st = st_ref[...]
    o = o + _dot_nt((q * jnp.exp(b)).astype(BF16), st.astype(BF16))
    b_last = b[R - 1:R, :]
    kd = (kf * jnp.exp(b_last - b)).astype(BF16)
    st_ref[...] = st * jnp.exp(b_last) + _dot(v.T.astype(BF16), kd) * bd_ref[...]

    ms = _dot((o * o).astype(BF16), gmean_ref[...])
    on = o * lax.rsqrt(ms + GN_EPS) * ng_ref[...]
    o_ref[0] = (on * _silu(g_ref[0].astype(F32))).astype(BF16)


def _hgrn2(hgf3, hgr3, lb_row, ng_row, R):
    B, S, _ = hgf3.shape
    idx = jnp.arange(R)
    tri = (idx[:, None] >= idx[None, :]).astype(F32)
    seg = jnp.stack([(idx[:, None] // (2 * m) == idx[None, :] // (2 * m)).astype(F32) for m in _hgrn_levels(R)])
    bd = _block_diag(jnp.ones((N_HEADS,), F32))
    gsum = bd.astype(BF16)
    gmean = (bd / HEAD_DIM).astype(BF16)
    col = lambda c: pl.BlockSpec((1, R, BRANCH_W), lambda b, j: (b, j, c))
    const2 = lambda a: pl.BlockSpec(a.shape, lambda b, j: (0, 0))
    return pl.pallas_call(
        _hgrn_kernel,
        out_shape=jax.ShapeDtypeStruct((B, S, BRANCH_W), BF16),
        grid=(B, S // R),
        in_specs=[col(0), col(0), col(1), col(2), const2(lb_row), const2(ng_row), const2(tri),
                  pl.BlockSpec(seg.shape, lambda b, j: (0, 0, 0)), const2(bd), const2(gsum), const2(gmean)],
        out_specs=pl.BlockSpec((1, R, BRANCH_W), lambda b, j: (b, j, 0)),
        scratch_shapes=[pltpu.VMEM((BRANCH_W, BRANCH_W), F32)],
        compiler_params=_cparams(("parallel", "arbitrary")),
        name="hgrn2",
    )(hgf3, hgr3, hgr3, hgr3, lb_row, ng_row, tri, seg, bd, gsum, gmean)


_R_W1, _R_W2, _R_E1, _R_E2, _R_RANK1, _R_RANK2 = range(6)


def _route_block(x1, wr_hi_ref, wr_lo_ref, br_ref, lstrict_ref, run_ref):
    tm = x1.shape[0]
    x_hi = x1.astype(BF16)
    x_lo = (x1 - x_hi.astype(F32)).astype(BF16)
    logits = (_dot(x_hi, wr_hi_ref[...]) + _dot(x_hi, wr_lo_ref[...]) + _dot(x_lo, wr_hi_ref[...])) + br_ref[...]
    lane = lax.broadcasted_iota(I32, (tm, LANES), 1).astype(F32)
    group = jnp.floor(lane * (1.0 / EXPERTS_PER_GROUP))
    e = jnp.exp(logits - jnp.max(logits, axis=-1, keepdims=True))
    p = e / jnp.sum(e, axis=-1, keepdims=True)
    p1 = jnp.max(p, axis=-1, keepdims=True)
    e1 = jnp.min(jnp.where(p == p1, lane, float(LANES)), axis=-1, keepdims=True)
    in_group = group == jnp.floor(e1 * (1.0 / EXPERTS_PER_GROUP))
    rest = jnp.where(in_group, jnp.where(lane == e1, -1.0, p), -1.0)
    p2 = jnp.max(rest, axis=-1, keepdims=True)
    e2 = jnp.min(jnp.where(rest == p2, lane, float(LANES)), axis=-1, keepdims=True)
    w1 = p1 / (p1 + p2)
    w2 = p2 / (p1 + p2)

    sel = jnp.where(lane == e1, 1.0, jnp.where(lane == e2, 1.0, 0.0))
    before = _dot(lstrict_ref[...], sel.astype(BF16)) + run_ref[0:1, :]
    rank1 = jnp.sum(jnp.where(lane == e1, before, 0.0), axis=-1, keepdims=True)
    rank2 = jnp.sum(jnp.where(lane == e2, before, 0.0), axis=-1, keepdims=True)
    run_ref[...] = run_ref[...] + jnp.sum(sel, axis=0, keepdims=True)

    info = jnp.zeros((tm, LANES), F32)
    for col, val in ((_R_W1, w1), (_R_W2, w2), (_R_E1, e1), (_R_E2, e2), (_R_RANK1, rank1), (_R_RANK2, rank2)):
        info = jnp.where(lane == float(col), val, info)
    pick = (lax.broadcasted_iota(I32, (8, LANES), 0) == lax.broadcasted_iota(I32, (8, LANES), 1)).astype(F32)
    rows = lax.dot_general(pick, info, _NT, precision=HIGHEST, preferred_element_type=F32)
    return info, rows


def _merge_kernel(x_ref, b0_ref, b1_ref, b2_ref, b3_ref, wgt_ref, bg_ref, wb_ref, wo_ref, lng_ref, lnb_ref,
                  wr_hi_ref, wr_lo_ref, br_ref, lstrict_ref, o_ref, info_ref, rows_ref, cnt_ref, run_ref, *, alpha):
    @pl.when(pl.program_id(0) == 0)
    def _():
        run_ref[...] = jnp.zeros(run_ref.shape, F32)

    x = x_ref[...]
    xb = x.astype(BF16)
    D = x.shape[1]
    mixed = jnp.zeros(x.shape, F32)
    for n, br in enumerate((b0_ref, b1_ref, b2_ref, b3_ref)):
        gate = jax.nn.sigmoid(_dot_nt(xb, wgt_ref[n * D:(n + 1) * D, :]) + bg_ref[:, n * D:(n + 1) * D])
        mixed = mixed + gate * _dot(br[...], wb_ref[n])
    z = alpha * x + _dot(mixed.astype(BF16), wo_ref[...])
    x1 = _layer_norm(z, lng_ref[...], lnb_ref[...])
    o_ref[...] = x1
    info, rows = _route_block(x1, wr_hi_ref, wr_lo_ref, br_ref, lstrict_ref, run_ref)
    info_ref[...] = info
    rows_ref[...] = rows
    cnt_ref[...] = run_ref[...]


def _merge(x2, branches, wgt, bg, wb, wo, lng, lnb, wr_pad, br_pad, alpha, tm):
    T, D = x2.shape
    const = lambda a: pl.BlockSpec(a.shape, lambda i: (0,) * a.ndim)
    br_spec = pl.BlockSpec((tm, BRANCH_W), lambda i: (i, 0))
    idx = jnp.arange(tm)
    lstrict = (idx[:, None] > idx[None, :]).astype(BF16)
    wr_hi = wr_pad.astype(BF16)
    wr_lo = (wr_pad - wr_hi.astype(F32)).astype(BF16)
    return pl.pallas_call(
        functools.partial(_merge_kernel, alpha=alpha),
        out_shape=[jax.ShapeDtypeStruct((T, D), F32), jax.ShapeDtypeStruct((T, LANES), F32),
                   jax.ShapeDtypeStruct((8, T), F32), jax.ShapeDtypeStruct((8, LANES), F32)],
        grid=(T // tm,),
        in_specs=[pl.BlockSpec((tm, D), lambda i: (i, 0)), br_spec, br_spec, br_spec, br_spec,
                  const(wgt), const(bg), const(wb), const(wo), const(lng), const(lnb),
                  const(wr_hi), const(wr_lo), const(br_pad), const(lstrict)],
        out_specs=[pl.BlockSpec((tm, D), lambda i: (i, 0)),
                   pl.BlockSpec((tm, LANES), lambda i: (i, 0)),
                   pl.BlockSpec((8, tm), lambda i: (0, i)),
                   pl.BlockSpec((8, LANES), lambda i: (0, 0))],
        scratch_shapes=[pltpu.VMEM((8, LANES), F32)],
        compiler_params=_cparams(("arbitrary",)),
        name="merge_ln_route",
    )(x2, *branches, wgt, bg, wb, wo, lng, lnb, wr_hi, wr_lo, br_pad, lstrict)


def _pos_kernel(off_ref, rows_ref, pos_ref):
    rows = rows_ref[...]
    e = rows[_R_E1:_R_E2 + 1, :]
    start = jnp.zeros(e.shape, F32)
    for ex in range(N_EXPERTS):
        start = jnp.where(e == float(ex), off_ref[ex].astype(F32), start)
    pos = (rows[_R_RANK1:_R_RANK2 + 1, :] + start).astype(I32)
    pos_ref[...] = jnp.zeros(pos_ref.shape, I32)
    pos_ref[0:2, :] = pos


def _positions(offsets, rows, tl):
    T = rows.shape[1]
    return pl.pallas_call(
        _pos_kernel,
        out_shape=jax.ShapeDtypeStruct((8, T), I32),
        grid_spec=pltpu.PrefetchScalarGridSpec(
            num_scalar_prefetch=1, grid=(T // tl,),
            in_specs=[pl.BlockSpec((8, tl), lambda i, off: (0, i))],
            out_specs=pl.BlockSpec((8, tl), lambda i, off: (0, i))),
        compiler_params=_cparams(("parallel",)),
        name="moe_positions",
    )(offsets, rows)


def _from_slab(ref, idx, rows, slab):
    return jnp.concatenate([ref[idx + (pl.ds(s, rows, stride=slab), slice(None))] for s in range(slab)], axis=1)


def _dispatch_kernel(pos1_ref, pos2_ref, x_ref, xs_in_ref, xs_ref, slab_ref, sem):
    del xs_in_ref
    td, D = x_ref.shape
    slab = D // LANES
    i = pl.program_id(0)
    slot = i % 2

    def wait_slot(s):
        for _ in range(2):
            pltpu.make_async_copy(slab_ref.at[s], xs_ref.at[pl.ds(0, td * slab)], sem.at[s]).wait()

    @pl.when(i >= 2)
    def _():
        wait_slot(slot)

    x = x_ref[...]
    for s in range(slab):
        slab_ref[slot, pl.ds(s, td, stride=slab), :] = x[:, s * LANES:(s + 1) * LANES]

    def issue(t, carry):
        src = slab_ref.at[slot, pl.ds(pl.multiple_of(t * slab, slab), slab)]
        for pos_ref in (pos1_ref, pos2_ref):
            dst = xs_ref.at[pl.ds(pl.multiple_of(pos_ref[t] * slab, slab), slab)]
            pltpu.make_async_copy(src, dst, sem.at[slot]).start()
        return carry

    lax.fori_loop(0, td, issue, 0)

    last = pl.num_programs(0) - 1

    @pl.when(i == last)
    def _():
        wait_slot(slot)

    @pl.when(jnp.logical_and(i == last, i >= 1))
    def _():
        wait_slot(1 - slot)


def _dispatch(pos1, pos2, x2, n_slots, td):
    T, D = x2.shape
    slab = D // LANES
    xs0 = jnp.zeros((n_slots * slab, LANES), F32)
    return pl.pallas_call(
        _dispatch_kernel,
        out_shape=jax.ShapeDtypeStruct((n_slots * slab, LANES), F32),
        grid=(T // td,),
        in_specs=[pl.BlockSpec((td,), lambda i: (i,), memory_space=pltpu.SMEM),
                  pl.BlockSpec((td,), lambda i: (i,), memory_space=pltpu.SMEM),
                  pl.BlockSpec((td, D), lambda i: (i, 0)),
                  pl.BlockSpec(memory_space=pl.ANY)],
        out_specs=pl.BlockSpec(memory_space=pl.ANY),
        scratch_shapes=[pltpu.VMEM((2, td * slab, LANES), F32), pltpu.SemaphoreType.DMA((2,))],
        input_output_aliases={3: 0},
        compiler_params=_cparams(("arbitrary",)),
        name="moe_dispatch",
    )(pos1, pos2, x2, xs0)


def _expert_kernel(be_ref, nu_ref, xs_ref, wi_ref, wo_ref, ys_ref):
    del be_ref
    used = pl.program_id(0) < nu_ref[0]

    @pl.when(used)
    def _():
        F = wo_ref.shape[1]
        slab = wo_ref.shape[2] // LANES
        bm = xs_ref.shape[0] // slab
        x = _from_slab(xs_ref, (), bm, slab)
        h = _dot(x.astype(BF16), wi_ref[0])
        act = _silu(h[:, :F]) * h[:, F:]
        y = _dot(act.astype(BF16), wo_ref[0])
        for s in range(slab):
            ys_ref[pl.ds(s, bm, stride=slab), :] = y[:, s * LANES:(s + 1) * LANES]

    @pl.when(jnp.logical_not(used))
    def _():
        ys_ref[...] = jnp.zeros(ys_ref.shape, F32)


def _experts(block_expert, n_used, xs, w_in, w_out, bm):
    E, D, F2 = w_in.shape
    slab = D // LANES
    n_blocks = xs.shape[0] // (bm * slab)
    row_blk = lambda i, be, nu: (jnp.maximum(jnp.minimum(i, nu[0] - 1), 0), 0)
    return pl.pallas_call(
        _expert_kernel,
        out_shape=jax.ShapeDtypeStruct(xs.shape, F32),
        grid_spec=pltpu.PrefetchScalarGridSpec(
            num_scalar_prefetch=2, grid=(n_blocks,),
            in_specs=[pl.BlockSpec((bm * slab, LANES), row_blk),
                      pl.BlockSpec((1, D, F2), lambda i, be, nu: (be[i], 0, 0)),
                      pl.BlockSpec((1, F2 // 2, D), lambda i, be, nu: (be[i], 0, 0))],
            out_specs=pl.BlockSpec((bm * slab, LANES), lambda i, be, nu: (i, 0))),
        compiler_params=_cparams(("arbitrary",)),
        name="moe_experts",
    )(block_expert, n_used, xs, w_in, w_out)


def _combine_kernel(pos1_ref, pos2_ref, nxt1_ref, nxt2_ref, info_ref, x_ref, ys_ref, lng_ref, lnb_ref, o_ref,
                    buf_ref, sem, *, alpha):
    tc, D = x_ref.shape
    slab = D // LANES
    i = pl.program_id(0)
    slot = i % 2

    def gather(p1_ref, p2_ref, into):
        def issue(t, carry):
            for k, pos_ref in enumerate((p1_ref, p2_ref)):
                src = ys_ref.at[pl.ds(pl.multiple_of(pos_ref[t] * slab, slab), slab)]
                dst = buf_ref.at[2 * into + k, pl.ds(pl.multiple_of(t * slab, slab), slab)]
                pltpu.make_async_copy(src, dst, sem.at[2 * into + k]).start()
            return carry

        lax.fori_loop(0, tc, issue, 0)

    @pl.when(i == 0)
    def _():
        gather(pos1_ref, pos2_ref, slot)

    @pl.when(i + 1 < pl.num_programs(0))
    def _():
        gather(nxt1_ref, nxt2_ref, 1 - slot)

    for k in range(2):
        pltpu.make_async_copy(ys_ref.at[pl.ds(0, tc * slab)], buf_ref.at[2 * slot + k], sem.at[2 * slot + k]).wait()
    info = info_ref[...]
    moe = (info[:, _R_W1:_R_W1 + 1] * _from_slab(buf_ref, (2 * slot,), tc, slab)
           + info[:, _R_W2:_R_W2 + 1] * _from_slab(buf_ref, (2 * slot + 1,), tc, slab))
    o_ref[...] = _layer_norm(alpha * x_ref[...] + moe, lng_ref[...], lnb_ref[...])


def _combine(pos1, pos2, info, x2, ys, lng, lnb, alpha, tc):
    T, D = x2.shape
    n = T // tc
    cur = pl.BlockSpec((tc,), lambda i: (i,), memory_space=pltpu.SMEM)
    nxt = pl.BlockSpec((tc,), lambda i: (jnp.minimum(i + 1, n - 1),), memory_space=pltpu.SMEM)
    return pl.pallas_call(
        functools.partial(_combine_kernel, alpha=alpha),
        out_shape=jax.ShapeDtypeStruct((T, D), F32),
        grid=(n,),
        in_specs=[cur, cur, nxt, nxt,
                  pl.BlockSpec((tc, LANES), lambda i: (i, 0)),
                  pl.BlockSpec((tc, D), lambda i: (i, 0)),
                  pl.BlockSpec(memory_space=pl.ANY),
                  pl.BlockSpec((1, D), lambda i: (0, 0)),
                  pl.BlockSpec((1, D), lambda i: (0, 0))],
        out_specs=pl.BlockSpec((tc, D), lambda i: (i, 0)),
        scratch_shapes=[pltpu.VMEM((4, tc * (D // LANES), LANES), F32), pltpu.SemaphoreType.DMA((4,))],
        compiler_params=_cparams(("arbitrary",)),
        name="moe_combine",
    )(pos1, pos2, pos1, pos2, info, x2, ys, lng, lnb)


def _moe(x2, info, rows, counts, w_e_in, w_e_out, lng, lnb, alpha, tiles):
    T, D = x2.shape
    bm = tiles["expert"]
    cnt = counts[0, :N_EXPERTS].astype(I32)
    padded = ((cnt + bm - 1) // bm) * bm
    ends = jnp.cumsum(padded)
    offsets = ends - padded
    n_blocks = (2 * T) // bm + N_EXPERTS
    n_used = (ends[-1] // bm).astype(I32)
    blk_start = jnp.arange(n_blocks, dtype=I32) * bm
    block_expert = jnp.sum((blk_start[:, None] >= ends[None, :]).astype(I32), axis=1)
    last_expert = jnp.sum((blk_start[n_used - 1] >= ends).astype(I32))
    block_expert = jnp.where(jnp.arange(n_blocks) < n_used, block_expert, last_expert).astype(I32)

    pos = _positions(offsets.astype(I32), rows, tiles["pos"])
    pos1, pos2 = pos[0], pos[1]
    xs = _dispatch(pos1, pos2, x2, n_blocks * bm, tiles["dispatch"])
    ys = _experts(block_expert, n_used.reshape(1), xs, w_e_in, w_e_out, bm)
    return _combine(pos1, pos2, info, x2, ys, lng, lnb, alpha, tiles["combine"])


def _tiles(B, S):
    T = B * S
    pick = lambda want, n: math.gcd(want, n)
    return dict(proj=pick(PROJ_ROWS, T), foxc=pick(FOX_BIAS_ROWS, S), fox=pick(FOX_BLOCK, S), mem=pick(MEM_ROWS, S),
                ret=pick(RET_ROWS, S), hgrn=pick(HGRN_ROWS, S), merge=pick(MERGE_ROWS, T),
                pos=pick(POS_LANES, T), dispatch=pick(DISPATCH_ROWS, T), expert=pick(EXPERT_ROWS, T),
                combine=pick(COMBINE_ROWS, T))


def _pack_input_weights(w_in, b_in):
    c = 3 * BRANCH_W
    n_mix = c + N_HEADS + 9 * BRANCH_W
    col_scale = jnp.ones((w_in.shape[-1],), F32).at[:BRANCH_W].set(HEAD_DIM ** -0.5 * LOG2E)
    wt = jnp.transpose(w_in * col_scale, (2, 0, 1))
    b = b_in * col_scale
    packed = []
    for l in range(w_in.shape[0]):
        w = wt[:, l, :]
        wt_mix = jnp.concatenate([w[:c + N_HEADS].astype(BF16), jnp.zeros((LANES - N_HEADS, w.shape[1]), BF16),
                                  w[c + N_HEADS:n_mix].astype(BF16)], axis=0)
        b_mix = jnp.concatenate([b[l, :c + N_HEADS], jnp.zeros((LANES - N_HEADS,), F32), b[l, c + N_HEADS:n_mix]])
        packed.append((wt_mix, b_mix[None, :], w[n_mix:].astype(BF16), b[l][None, n_mix:]))
    return packed


def kernel(x, mem, w_in, b_in, w_mem_kv, hgrn_lb, hgrn_norm_g, w_branch, w_out, ln_g, ln_b, w_router, b_router,
           w_e_in, w_e_out):
    B, S, D = x.shape
    T = B * S
    depth = w_in.shape[0]
    alpha = (2.0 * depth) ** 0.25
    tiles = _tiles(B, S)

    lb_all = jax.nn.softmax(hgrn_lb.astype(F32), axis=0)
    lb_all = jnp.cumsum(lb_all, axis=0) - lb_all[0:1]
    wr_pad = jnp.concatenate([w_router, jnp.zeros((D, LANES - N_EXPERTS), F32)], axis=1)
    br_pad = jnp.concatenate([b_router, jnp.full((LANES - N_EXPERTS,), MASK_VALUE, F32)])[None, :]

    x2 = x.reshape(T, D)
    packed = _pack_input_weights(w_in, b_in)
    for l in range(depth):
        wt_mix, b_mix, wt_gate, b_gate = packed[l]
        qkv, ff, ret, hgf, hgr, mq = _proj(x2, wt_mix, b_mix, tiles["proj"])
        r3 = lambda a: a.reshape(B, S, a.shape[-1])
        kb = _fox_key_bias(r3(ff), tiles["foxc"])
        o_fox = _fox_attention(r3(qkv), kb, tiles["fox"])
        o_ret = _retention(r3(ret), tiles["ret"])
        o_hg = _hgrn2(r3(hgf), r3(hgr), lb_all[l][None, :], jnp.tile(hgrn_norm_g[l], N_HEADS)[None, :],
                      tiles["hgrn"])
        o_mem = _mem_attention(r3(mq), mem, w_mem_kv[l].astype(BF16), tiles["mem"])
        branches = [o.reshape(T, BRANCH_W) for o in (o_fox, o_ret, o_hg, o_mem)]
        x2, info, rows, counts = _merge(x2, branches, wt_gate, b_gate, w_branch[l].astype(BF16),
                                        w_out[l].astype(BF16), ln_g[l, 0][None, :], ln_b[l, 0][None, :],
                                        wr_pad, br_pad, alpha, tiles["merge"])
        x2 = _moe(x2, info, rows, counts, w_e_in[l].astype(BF16), w_e_out[l].astype(BF16),
                  ln_g[l, 1][None, :], ln_b[l, 1][None, :], alpha, tiles)
    return x2.reshape(B, S, D)
```

```python
import functools
import math

import jax
import jax.numpy as jnp
from jax import lax
from jax.experimental import pallas as pl
from jax.experimental.pallas import tpu as pltpu

F32 = jnp.float32
BF16 = jnp.bfloat16
I32 = jnp.int32
HIGHEST = lax.Precision.HIGHEST

N_HEADS = 4
HEAD_DIM = 64
BRANCH_W = N_HEADS * HEAD_DIM
N_BRANCH = 4
RET_CHUNK = 64
ROPE_BASE = 10000.0
N_EXPERTS = 16
EXPERTS_PER_GROUP = 4
LN_EPS = 1e-5
GN_EPS = 1e-6
MASK_VALUE = -1e30
LOG2E = math.log2(math.e)

LANES = 128
V7X_VMEM_LIMIT_BYTES = 52 * 1024 * 1024

MIXER_ROWS = 512
FOX_BLOCK = 512
RET_ROWS = 256
HGRN_ROWS = 128
HGRN_SUB = 8
MERGE_ROWS = 512
POS_LANES = 2048
DISPATCH_ROWS = 512
EXPERT_ROWS = 512
COMBINE_ROWS = 256

_W_FOX = 3 * BRANCH_W
_C_FOXF = _W_FOX
_C_RET = _C_FOXF + LANES
_C_HGF = _C_RET + 4 * BRANCH_W
_C_HGR = _C_HGF + BRANCH_W
_C_MQ = _C_HGR + 3 * BRANCH_W
_C_END = _C_MQ + BRANCH_W

_NT = (((1,), (1,)), ((), ()))


def _cparams(sem):
    return pltpu.CompilerParams(dimension_semantics=sem, vmem_limit_bytes=V7X_VMEM_LIMIT_BYTES)


def _head_id(shape):
    return lax.broadcasted_iota(I32, shape, len(shape) - 1) // HEAD_DIM


def _dot(a, b):
    return jnp.dot(a, b, preferred_element_type=F32)


def _dot_nt(a, b):
    return lax.dot_general(a, b, _NT, preferred_element_type=F32)


def _silu(x):
    return x * jax.nn.sigmoid(x)


def _layer_norm(z, g, b):
    mu = jnp.mean(z, axis=-1, keepdims=True)
    d = z - mu
    var = jnp.mean(d * d, axis=-1, keepdims=True)
    return d * lax.rsqrt(var + LN_EPS) * g + b


_FOX_BIAS_PARTS = 3


def _fox_spare_lane(h):
    return HEAD_DIM * ((h + 1) % N_HEADS)


def _fox_place_table():
    lane = jnp.arange(BRANCH_W)
    head = jnp.arange(LANES)
    dest = jnp.where(head < N_HEADS, HEAD_DIM * ((head + 1) % N_HEADS), -BRANCH_W)
    return jnp.stack([(lane[None, :] == dest[:, None] + part) for part in range(_FOX_BIAS_PARTS)]).astype(BF16)


def _key_bias_block(ff, tri, place_ref, carry_ref):
    ls = jax.nn.log_sigmoid(ff) * LOG2E
    cs = jnp.dot(tri, ls, precision=HIGHEST, preferred_element_type=F32) + carry_ref[0:1, :]
    carry_ref[...] = jnp.broadcast_to(cs[ff.shape[0] - 1:, :], carry_ref.shape)
    rest = -cs
    kb = jnp.zeros((ff.shape[0], BRANCH_W), F32)
    for part in range(_FOX_BIAS_PARTS):
        piece = rest.astype(BF16)
        rest = rest - piece.astype(F32)
        kb = kb + _dot(piece, place_ref[part])
    return kb.astype(BF16)


def _fox_lane_table():
    lane = jnp.arange(BRANCH_W)
    rows = []
    for h in range(N_HEADS):
        spare = _fox_spare_lane(h)
        bias = (lane >= spare) & (lane < spare + _FOX_BIAS_PARTS)
        rows += [lane // HEAD_DIM == h, bias, ~bias, lane == spare]
    return jnp.stack(rows).astype(F32)


def _fox_kernel(q_ref, k_ref, v_ref, kb_ref, lanes_ref, o_ref, qh_ref, m_ref, acc_ref):
    qi = pl.program_id(1)
    ki = pl.program_id(2)
    blk = q_ref.shape[1]

    def pattern(h, r):
        return lanes_ref[4 * h + r:4 * h + r + 1, :].astype(BF16)

    @pl.when(ki == 0)
    def _():
        m_ref[...] = jnp.full(m_ref.shape, MASK_VALUE, F32)
        acc_ref[...] = jnp.zeros(acc_ref.shape, F32)
        q = q_ref[0]
        for h in range(N_HEADS):
            qh_ref[h] = q * pattern(h, 0) + pattern(h, 1)

    def sweep(masked):
        k = k_ref[0]
        v = v_ref[0]
        kb = kb_ref[0]
        if masked:
            causal = lax.broadcasted_iota(I32, (blk, blk), 0) >= lax.broadcasted_iota(I32, (blk, blk), 1)
        for h in range(N_HEADS):
            qh = qh_ref[h]
            kh = k * pattern(h, 2) + kb * pattern(h, 1)
            ones_lane = pattern(h, 3)
            vh = v * (1 - ones_lane) + ones_lane
            s = _dot_nt(qh, kh)
            if masked:
                s = jnp.where(causal, s, MASK_VALUE)
            m_prev = m_ref[h]
            m_new = jnp.maximum(m_prev, jnp.max(s, axis=-1, keepdims=True))
            alpha = jnp.exp2(m_prev - m_new)
            p = jnp.exp2(s - jnp.concatenate([m_new] * (blk // LANES), axis=1))
            m_ref[h] = m_new
            acc_ref[h] = acc_ref[h] * jnp.concatenate([alpha] * (BRANCH_W // LANES), axis=1) + _dot(p.astype(BF16), vh)

    @pl.when(ki < qi)
    def _():
        sweep(False)

    @pl.when(ki == qi)
    def _():
        sweep(True)
        lane = lax.broadcasted_iota(I32, (blk, BRANCH_W), 1)
        out = jnp.zeros((blk, BRANCH_W), F32)
        for h in range(N_HEADS):
            acc = acc_ref[h]
            spare = _fox_spare_lane(h)
            out = jnp.where((lane // HEAD_DIM) == h, acc / acc[:, spare:spare + 1], out)
        o_ref[0] = out.astype(BF16)


def _fox_attention(qkv3, kb, blk):
    B, S, _ = qkv3.shape
    n = S // blk
    kv_blk = lambda c: pl.BlockSpec((1, blk, BRANCH_W), lambda b, qi, ki: (b, jnp.minimum(ki, qi), c))
    lanes = _fox_lane_table()
    return pl.pallas_call(
        _fox_kernel,
        out_shape=jax.ShapeDtypeStruct((B, S, BRANCH_W), BF16),
        grid=(B, n, n),
        in_specs=[pl.BlockSpec((1, blk, BRANCH_W), lambda b, qi, ki: (b, qi, 0)), kv_blk(1), kv_blk(2),
                  pl.BlockSpec((1, blk, BRANCH_W), lambda b, qi, ki: (b, jnp.minimum(ki, qi), 0)),
                  pl.BlockSpec(lanes.shape, lambda b, qi, ki: (0, 0))],
        out_specs=pl.BlockSpec((1, blk, BRANCH_W), lambda b, qi, ki: (b, qi, 0)),
        scratch_shapes=[pltpu.VMEM((N_HEADS, blk, BRANCH_W), BF16),
                        pltpu.VMEM((N_HEADS, blk, LANES), F32),
                        pltpu.VMEM((N_HEADS, blk, BRANCH_W), F32)],
        compiler_params=_cparams(("parallel", "parallel", "arbitrary")),
        name="fox_attention",
    )(qkv3, qkv3, qkv3, kb, lanes)


def _mem_block(q, mk, mv):
    rows = q.shape[0]
    q = q * (HEAD_DIM ** -0.5)
    hid = _head_id((rows, BRANCH_W))
    out = jnp.zeros((rows, BRANCH_W), F32)
    for h in range(N_HEADS):
        qh = jnp.where(hid == h, q, jnp.zeros_like(q))
        s = _dot_nt(qh, mk)
        p = jnp.exp(s - jnp.max(s, axis=-1, keepdims=True))
        pv = _dot(p.astype(BF16), mv)
        out = jnp.where(hid == h, pv / jnp.sum(p, axis=-1, keepdims=True), out)
    return out.astype(BF16)


def _block_diag(val_per_head):
    hid = jnp.arange(BRANCH_W) // HEAD_DIM
    same = hid[:, None] == hid[None, :]
    return jnp.where(same, jnp.asarray(val_per_head, F32)[hid][:, None], 0.0)


def _rope_tables(S):
    half = HEAD_DIM // 2
    inv = ROPE_BASE ** (-2.0 * jnp.arange(half, dtype=F32) / HEAD_DIM)
    ang = jnp.arange(S, dtype=F32)[:, None] * inv[None, :]
    cos = jnp.tile(jnp.cos(ang), (1, 2 * N_HEADS))
    sin = jnp.tile(jnp.sin(ang), (1, 2 * N_HEADS))
    d = jnp.arange(BRANCH_W)
    lo = (d % HEAD_DIM) < half
    perm = jnp.where(lo[None, :] & (d[:, None] == d[None, :] + half), -1.0, 0.0)
    perm = perm + jnp.where((~lo)[None, :] & (d[:, None] == d[None, :] - half), 1.0, 0.0)
    return cos, sin, perm.astype(BF16)


def _ret_tables(R):
    log_gamma = jnp.log1p(-jnp.exp2(-5.0 - jnp.arange(N_HEADS, dtype=F32)))
    idx = jnp.arange(R, dtype=F32)
    chunk = jnp.arange(R) // RET_CHUNK
    dist = jnp.abs(idx[:, None] - idx[None, :])
    visible = chunk[None, :] <= chunk[:, None]
    decay = jnp.where(visible[None], jnp.exp(log_gamma[:, None, None] * dist[None]), 0.0)
    lanes_lg = jnp.repeat(log_gamma, HEAD_DIM)[None, :]
    qdec = jnp.exp(lanes_lg * (idx[:, None] + 1.0))
    kdec = jnp.exp(lanes_lg * (R - 1.0 - idx[:, None]))
    sdec = _block_diag(jnp.exp(log_gamma * R))
    return decay, qdec, kdec, sdec


def _ret_block(qb, kb, v, g, cos, sin, perm, decay_ref, qdec, kdec, sdec, bd, gmean, st_ref):
    R = qb.shape[0]
    qr = qb.astype(F32) * cos + _dot(qb, perm) * sin
    kr = (kb.astype(F32) * cos + _dot(kb, perm) * sin) * (HEAD_DIM ** -0.5)
    qrb = qr.astype(BF16)
    krb = kr.astype(BF16)
    hid = _head_id((R, BRANCH_W))

    st = st_ref[...]
    o = _dot((qr * qdec).astype(BF16), st.astype(BF16))
    for h in range(N_HEADS):
        qh = jnp.where(hid == h, qrb, jnp.zeros_like(qrb))
        a = _dot_nt(qh, krb) * decay_ref[h]
        o = o + jnp.where(hid == h, _dot(a.astype(BF16), v), 0.0)
    kd_t = (kr * kdec).T.astype(BF16)
    st_ref[...] = st * sdec + _dot(kd_t, v) * bd

    mu = _dot(o.astype(BF16), gmean)
    d = o - mu
    var = _dot((d * d).astype(BF16), gmean)
    on = d * lax.rsqrt(var + GN_EPS)
    return (on * _silu(g.astype(F32))).astype(BF16)


def _hgrn_levels(R):
    levels = []
    m = HGRN_SUB
    while 2 * m <= R:
        levels.append(m)
        m *= 2
    return tuple(levels)


def _hgrn_block(zf, vb, qb, g, lb, ng, tri, seg_ref, bd, gsum, gmean, st_ref):
    R = zf.shape[0]
    W = BRANCH_W
    kf = (1.0 - lb) * jax.nn.sigmoid(-zf)
    logf = jnp.log1p(-kf)
    b = jnp.dot(tri, logf, precision=HIGHEST, preferred_element_type=F32)
    q = qb.astype(F32)
    v = vb.astype(F32)
    hid = _head_id((R, W))
    row = lax.broadcasted_iota(I32, (R, W), 0)

    o = _dot((q * kf).astype(BF16), gsum) * v
    sub = row % HGRN_SUB
    for d in range(1, HGRN_SUB):
        e = jnp.exp(jnp.minimum(b - pltpu.roll(b, d, 0), 0.0))
        w = jnp.where(sub >= d, q * pltpu.roll(kf, d, 0) * e, 0.0)
        o = o + _dot(w.astype(BF16), gsum) * pltpu.roll(v, d, 0)

    a_heads = [jnp.zeros((R, R), F32) for _ in range(N_HEADS)]
    for li, m in enumerate(_hgrn_levels(R)):
        nseg = R // (2 * m)
        bnd = jnp.broadcast_to(b.reshape(nseg, 2 * m, W)[:, m - 1:m, :], (nseg, 2 * m, W)).reshape(R, W)
        second = (row % (2 * m)) >= m
        qm = jnp.where(second, q * jnp.exp(jnp.minimum(b - bnd, 0.0)), 0.0).astype(BF16)
        km = jnp.where(second, 0.0, kf * jnp.exp(jnp.minimum(bnd - b, 0.0))).astype(BF16)
        for h in range(N_HEADS):
            qh = jnp.where(hid == h, qm, jnp.zeros_like(qm))
            a_heads[h] = a_heads[h] + _dot_nt(qh, km) * seg_ref[li]
    for h in range(N_HEADS):
        o = o + jnp.where(hid == h, _dot(a_heads[h].astype(BF16), vb), 0.0)

    st = st_ref[...]
    o = o + _dot_nt((q * jnp.exp(b)).astype(BF16), st.astype(BF16))
    b_last = b[R - 1:R, :]
    kd = (kf * jnp.exp(b_last - b)).astype(BF16)
    st_ref[...] = st * jnp.exp(b_last) + _dot(v.T.astype(BF16), kd) * bd

    ms = _dot((o * o).astype(BF16), gmean)
    on = o * lax.rsqrt(ms + GN_EPS) * ng
    return (on * _silu(g.astype(F32))).astype(BF16)


def _mixer_kernel(x_ref, wt_ref, b_ref, mem_ref, wkv_ref, cos_ref, sin_ref, perm_ref, decay_ref, qdec_ref,
                  kdec_ref, sdec_ref, bd_ref, gsum_ref, gmean_ref, lb_ref, ng_ref, tri_ref, seg_ref, place_ref,
                  qkv_ref, kb_ref, oret_ref, ohg_ref, omem_ref,
                  kv_ref, ret_st_ref, hg_st_ref, carry_ref):
    @pl.when(pl.program_id(1) == 0)
    def _():
        kv_ref[...] = _dot(mem_ref[0].astype(BF16), wkv_ref[...]).astype(BF16)
        ret_st_ref[...] = jnp.zeros(ret_st_ref.shape, F32)
        hg_st_ref[...] = jnp.zeros(hg_st_ref.shape, F32)
        carry_ref[...] = jnp.zeros(carry_ref.shape, F32)

    xb = x_ref[0].astype(BF16)
    tm = xb.shape[0]
    W = BRANCH_W

    def seg(lo, hi):
        return _dot_nt(xb, wt_ref[lo:hi, :]) + b_ref[:, lo:hi]

    hgf = seg(_C_HGF, _C_HGR)
    hgr = seg(_C_HGR, _C_MQ).astype(BF16)
    bd = bd_ref[...]
    gmean = gmean_ref[...]
    tri = tri_ref[...]
    R = tri.shape[0]
    for r in range(tm // R):
        rows = slice(r * R, (r + 1) * R)
        ohg_ref[0, rows, :] = _hgrn_block(hgf[rows], hgr[rows, :W], hgr[rows, W:2 * W], hgr[rows, 2 * W:],
                                          lb_ref[...], ng_ref[...], tri, seg_ref, bd, gsum_ref[...], gmean, hg_st_ref)

    ff = seg(_C_FOXF, _C_RET)
    for r in range(tm // R):
        rows = slice(r * R, (r + 1) * R)
        kb_ref[0, rows, :] = _key_bias_block(ff[rows], tri, place_ref, carry_ref)
    qkv_ref[0] = seg(0, _C_FOXF).astype(BF16)

    ret = seg(_C_RET, _C_HGF).astype(BF16)
    RR = qdec_ref.shape[0]
    for r in range(tm // RR):
        rows = slice(r * RR, (r + 1) * RR)
        oret_ref[0, rows, :] = _ret_block(ret[rows, :W], ret[rows, W:2 * W], ret[rows, 2 * W:3 * W], ret[rows, 3 * W:],
                                          cos_ref[rows, :], sin_ref[rows, :], perm_ref[...], decay_ref, qdec_ref[...],
                                          kdec_ref[...], sdec_ref[...], bd, gmean, ret_st_ref)

    mq = seg(_C_MQ, _C_END).astype(BF16)
    omem_ref[0] = _mem_block(mq, kv_ref[:, :W], kv_ref[:, W:])


def _mixer(x3, wt_mix, b_mix, mem, w_kv, lb_row, ng_row, tm, ret_rows, hgrn_rows):
    B, S, D = x3.shape
    M = mem.shape[1]
    cos, sin, perm = _rope_tables(S)
    decay, qdec, kdec, sdec = _ret_tables(ret_rows)
    bd = _block_diag(jnp.ones((N_HEADS,), F32))
    gsum = bd.astype(BF16)
    gmean = (bd / HEAD_DIM).astype(BF16)
    idx = jnp.arange(hgrn_rows)
    tri = (idx[:, None] >= idx[None, :]).astype(F32)
    seg = jnp.stack([(idx[:, None] // (2 * m) == idx[None, :] // (2 * m)).astype(F32)
                     for m in _hgrn_levels(hgrn_rows)])
    place = _fox_place_table()
    const = lambda a: pl.BlockSpec(a.shape, lambda b, j: (0,) * a.ndim)
    tile = lambda w: pl.BlockSpec((1, tm, w), lambda b, j: (b, j, 0))
    pos_tab = pl.BlockSpec((tm, BRANCH_W), lambda b, j: (j, 0))
    return pl.pallas_call(
        _mixer_kernel,
        out_shape=[jax.ShapeDtypeStruct((B, S, _C_FOXF), BF16)] + [jax.ShapeDtypeStruct((B, S, BRANCH_W), BF16)] * 4,
        grid=(B, S // tm),
        in_specs=[tile(D), const(wt_mix), const(b_mix),
                  pl.BlockSpec((1, M, D), lambda b, j: (b, 0, 0)), const(w_kv),
                  pos_tab, pos_tab, const(perm), const(decay), const(qdec), const(kdec), const(sdec),
                  const(bd), const(gsum), const(gmean), const(lb_row), const(ng_row), const(tri), const(seg),
                  const(place)],
        out_specs=[tile(_C_FOXF)] + [tile(BRANCH_W)] * 4,
        scratch_shapes=[pltpu.VMEM((M, 2 * BRANCH_W), BF16),
                        pltpu.VMEM((BRANCH_W, BRANCH_W), F32),
                        pltpu.VMEM((BRANCH_W, BRANCH_W), F32),
                        pltpu.VMEM((8, LANES), F32)],
        compiler_params=_cparams(("parallel", "arbitrary")),
        name="mixer",
    )(x3, wt_mix, b_mix, mem, w_kv, cos, sin, perm, decay, qdec, kdec, sdec, bd, gsum, gmean, lb_row, ng_row,
      tri, seg, place)


_R_W1, _R_W2, _R_E1, _R_E2, _R_RANK1, _R_RANK2 = range(6)


def _route_block(x1, wr_hi_ref, wr_lo_ref, br_ref, lstrict_ref, run_ref):
    tm = x1.shape[0]
    x_hi = x1.astype(BF16)
    x_lo = (x1 - x_hi.astype(F32)).astype(BF16)
    logits = (_dot(x_hi, wr_hi_ref[...]) + _dot(x_hi, wr_lo_ref[...]) + _dot(x_lo, wr_hi_ref[...])) + br_ref[...]
    lane = lax.broadcasted_iota(I32, (tm, LANES), 1).astype(F32)
    group = jnp.floor(lane * (1.0 / EXPERTS_PER_GROUP))
    e = jnp.exp(logits - jnp.max(logits, axis=-1, keepdims=True))
    p = e / jnp.sum(e, axis=-1, keepdims=True)
    p1 = jnp.max(p, axis=-1, keepdims=True)
    e1 = jnp.min(jnp.where(p == p1, lane, float(LANES)), axis=-1, keepdims=True)
    in_group = group == jnp.floor(e1 * (1.0 / EXPERTS_PER_GROUP))
    rest = jnp.where(in_group, jnp.where(lane == e1, -1.0, p), -1.0)
    p2 = jnp.max(rest, axis=-1, keepdims=True)
    e2 = jnp.min(jnp.where(rest == p2, lane, float(LANES)), axis=-1, keepdims=True)
    w1 = p1 / (p1 + p2)
    w2 = p2 / (p1 + p2)

    sel = jnp.where(lane == e1, 1.0, jnp.where(lane == e2, 1.0, 0.0))
    before = _dot(lstrict_ref[...], sel.astype(BF16)) + run_ref[0:1, :]
    rank1 = jnp.sum(jnp.where(lane == e1, before, 0.0), axis=-1, keepdims=True)
    rank2 = jnp.sum(jnp.where(lane == e2, before, 0.0), axis=-1, keepdims=True)
    run_ref[...] = run_ref[...] + jnp.sum(sel, axis=0, keepdims=True)

    info = jnp.zeros((tm, LANES), F32)
    for col, val in ((_R_W1, w1), (_R_W2, w2), (_R_E1, e1), (_R_E2, e2), (_R_RANK1, rank1), (_R_RANK2, rank2)):
        info = jnp.where(lane == float(col), val, info)
    pick = (lax.broadcasted_iota(I32, (8, LANES), 0) == lax.broadcasted_iota(I32, (8, LANES), 1)).astype(F32)
    rows = lax.dot_general(pick, info, _NT, precision=HIGHEST, preferred_element_type=F32)
    return info, rows


def _merge_kernel(x_ref, b0_ref, b1_ref, b2_ref, b3_ref, wgt_ref, bg_ref, wb_ref, wo_ref, lng_ref, lnb_ref,
                  wr_hi_ref, wr_lo_ref, br_ref, lstrict_ref, o_ref, info_ref, rows_ref, cnt_ref, run_ref, *, alpha):
    @pl.when(pl.program_id(0) == 0)
    def _():
        run_ref[...] = jnp.zeros(run_ref.shape, F32)

    x = x_ref[...]
    xb = x.astype(BF16)
    D = x.shape[1]
    mixed = jnp.zeros(x.shape, F32)
    for n, br in enumerate((b0_ref, b1_ref, b2_ref, b3_ref)):
        gate = jax.nn.sigmoid(_dot_nt(xb, wgt_ref[n * D:(n + 1) * D, :]) + bg_ref[:, n * D:(n + 1) * D])
        mixed = mixed + gate * _dot(br[...], wb_ref[n])
    z = alpha * x + _dot(mixed.astype(BF16), wo_ref[...])
    x1 = _layer_norm(z, lng_ref[...], lnb_ref[...])
    o_ref[...] = x1
    info, rows = _route_block(x1, wr_hi_ref, wr_lo_ref, br_ref, lstrict_ref, run_ref)
    info_ref[...] = info
    rows_ref[...] = rows
    cnt_ref[...] = run_ref[...]


def _merge(x2, branches, wgt, bg, wb, wo, lng, lnb, wr_pad, br_pad, alpha, tm):
    T, D = x2.shape
    const = lambda a: pl.BlockSpec(a.shape, lambda i: (0,) * a.ndim)
    br_spec = pl.BlockSpec((tm, BRANCH_W), lambda i: (i, 0))
    idx = jnp.arange(tm)
    lstrict = (idx[:, None] > idx[None, :]).astype(BF16)
    wr_hi = wr_pad.astype(BF16)
    wr_lo = (wr_pad - wr_hi.astype(F32)).astype(BF16)
    return pl.pallas_call(
        functools.partial(_merge_kernel, alpha=alpha),
        out_shape=[jax.ShapeDtypeStruct((T, D), F32), jax.ShapeDtypeStruct((T, LANES), F32),
                   jax.ShapeDtypeStruct((8, T), F32), jax.ShapeDtypeStruct((8, LANES), F32)],
        grid=(T // tm,),
        in_specs=[pl.BlockSpec((tm, D), lambda i: (i, 0)), br_spec, br_spec, br_spec, br_spec,
                  const(wgt), const(bg), const(wb), const(wo), const(lng), const(lnb),
                  const(wr_hi), const(wr_lo), const(br_pad), const(lstrict)],
        out_specs=[pl.BlockSpec((tm, D), lambda i: (i, 0)),
                   pl.BlockSpec((tm, LANES), lambda i: (i, 0)),
                   pl.BlockSpec((8, tm), lambda i: (0, i)),
                   pl.BlockSpec((8, LANES), lambda i: (0, 0))],
        scratch_shapes=[pltpu.VMEM((8, LANES), F32)],
        compiler_params=_cparams(("arbitrary",)),
        name="merge_ln_route",
    )(x2, *branches, wgt, bg, wb, wo, lng, lnb, wr_hi, wr_lo, br_pad, lstrict)


def _pos_kernel(off_ref, rows_ref, pos_ref):
    rows = rows_ref[...]
    e = rows[_R_E1:_R_E2 + 1, :]
    start = jnp.zeros(e.shape, F32)
    for ex in range(N_EXPERTS):
        start = jnp.where(e == float(ex), off_ref[ex].astype(F32), start)
    pos = (rows[_R_RANK1:_R_RANK2 + 1, :] + start).astype(I32)
    pos_ref[...] = jnp.zeros(pos_ref.shape, I32)
    pos_ref[0:2, :] = pos


def _positions(offsets, rows, tl):
    T = rows.shape[1]
    return pl.pallas_call(
        _pos_kernel,
        out_shape=jax.ShapeDtypeStruct((8, T), I32),
        grid_spec=pltpu.PrefetchScalarGridSpec(
            num_scalar_prefetch=1, grid=(T // tl,),
            in_specs=[pl.BlockSpec((8, tl), lambda i, off: (0, i))],
            out_specs=pl.BlockSpec((8, tl), lambda i, off: (0, i))),
        compiler_params=_cparams(("parallel",)),
        name="moe_positions",
    )(offsets, rows)


def _from_slab(ref, idx, rows, slab):
    return jnp.concatenate([ref[idx + (pl.ds(s, rows, stride=slab), slice(None))] for s in range(slab)], axis=1)


def _dispatch_kernel(pos1_ref, pos2_ref, x_ref, xs_in_ref, xs_ref, slab_ref, sem):
    del xs_in_ref
    td, D = x_ref.shape
    slab = D // LANES
    i = pl.program_id(0)
    slot = i % 2

    def wait_slot(s):
        for _ in range(2):
            pltpu.make_async_copy(slab_ref.at[s], xs_ref.at[pl.ds(0, td * slab)], sem.at[s]).wait()

    @pl.when(i >= 2)
    def _():
        wait_slot(slot)

    x = x_ref[...]
    for s in range(slab):
        slab_ref[slot, pl.ds(s, td, stride=slab), :] = x[:, s * LANES:(s + 1) * LANES]

    def issue(t, carry):
        src = slab_ref.at[slot, pl.ds(pl.multiple_of(t * slab, slab), slab)]
        for pos_ref in (pos1_ref, pos2_ref):
            dst = xs_ref.at[pl.ds(pl.multiple_of(pos_ref[t] * slab, slab), slab)]
            pltpu.make_async_copy(src, dst, sem.at[slot]).start()
        return carry

    lax.fori_loop(0, td, issue, 0)

    last = pl.num_programs(0) - 1

    @pl.when(i == last)
    def _():
        wait_slot(slot)

    @pl.when(jnp.logical_and(i == last, i >= 1))
    def _():
        wait_slot(1 - slot)


def _dispatch(pos1, pos2, x2, n_slots, td):
    T, D = x2.shape
    slab = D // LANES
    xs0 = jnp.zeros((n_slots * slab, LANES), F32)
    return pl.pallas_call(
        _dispatch_kernel,
        out_shape=jax.ShapeDtypeStruct((n_slots * slab, LANES), F32),
        grid=(T // td,),
        in_specs=[pl.BlockSpec((td,), lambda i: (i,), memory_space=pltpu.SMEM),
                  pl.BlockSpec((td,), lambda i: (i,), memory_space=pltpu.SMEM),
                  pl.BlockSpec((td, D), lambda i: (i, 0)),
                  pl.BlockSpec(memory_space=pl.ANY)],
        out_specs=pl.BlockSpec(memory_space=pl.ANY),
        scratch_shapes=[pltpu.VMEM((2, td * slab, LANES), F32), pltpu.SemaphoreType.DMA((2,))],
        input_output_aliases={3: 0},
        compiler_params=_cparams(("arbitrary",)),
        name="moe_dispatch",
    )(pos1, pos2, x2, xs0)


def _expert_kernel(be_ref, nu_ref, xs_ref, wi_ref, wo_ref, ys_ref):
    del be_ref
    used = pl.program_id(0) < nu_ref[0]

    @pl.when(used)
    def _():
        F = wo_ref.shape[1]
        slab = wo_ref.shape[2] // LANES
        bm = xs_ref.shape[0] // slab
        x = _from_slab(xs_ref, (), bm, slab)
        h = _dot(x.astype(BF16), wi_ref[0])
        act = _silu(h[:, :F]) * h[:, F:]
        y = _dot(act.astype(BF16), wo_ref[0])
        for s in range(slab):
            ys_ref[pl.ds(s, bm, stride=slab), :] = y[:, s * LANES:(s + 1) * LANES]

    @pl.when(jnp.logical_not(used))
    def _():
        ys_ref[...] = jnp.zeros(ys_ref.shape, F32)


def _experts(block_expert, n_used, xs, w_in, w_out, bm):
    E, D, F2 = w_in.shape
    slab = D // LANES
    n_blocks = xs.shape[0] // (bm * slab)
    row_blk = lambda i, be, nu: (jnp.maximum(jnp.minimum(i, nu[0] - 1), 0), 0)
    return pl.pallas_call(
        _expert_kernel,
        out_shape=jax.ShapeDtypeStruct(xs.shape, F32),
        grid_spec=pltpu.PrefetchScalarGridSpec(
            num_scalar_prefetch=2, grid=(n_blocks,),
            in_specs=[pl.BlockSpec((bm * slab, LANES), row_blk),
                      pl.BlockSpec((1, D, F2), lambda i, be, nu: (be[i], 0, 0)),
                      pl.BlockSpec((1, F2 // 2, D), lambda i, be, nu: (be[i], 0, 0))],
            out_specs=pl.BlockSpec((bm * slab, LANES), lambda i, be, nu: (i, 0))),
        compiler_params=_cparams(("arbitrary",)),
        name="moe_experts",
    )(block_expert, n_used, xs, w_in, w_out)


def _combine_kernel(pos1_ref, pos2_ref, nxt1_ref, nxt2_ref, info_ref, x_ref, ys_ref, lng_ref, lnb_ref, o_ref,
                    buf_ref, sem, *, alpha):
    tc, D = x_ref.shape
    slab = D // LANES
    i = pl.program_id(0)
    slot = i % 2

    def gather(p1_ref, p2_ref, into):
        def issue(t, carry):
            for k, pos_ref in enumerate((p1_ref, p2_ref)):
                src = ys_ref.at[pl.ds(pl.multiple_of(pos_ref[t] * slab, slab), slab)]
                dst = buf_ref.at[2 * into + k, pl.ds(pl.multiple_of(t * slab, slab), slab)]
                pltpu.make_async_copy(src, dst, sem.at[2 * into + k]).start()
            return carry

        lax.fori_loop(0, tc, issue, 0)

    @pl.when(i == 0)
    def _():
        gather(pos1_ref, pos2_ref, slot)

    @pl.when(i + 1 < pl.num_programs(0))
    def _():
        gather(nxt1_ref, nxt2_ref, 1 - slot)

    for k in range(2):
        pltpu.make_async_copy(ys_ref.at[pl.ds(0, tc * slab)], buf_ref.at[2 * slot + k], sem.at[2 * slot + k]).wait()
    info = info_ref[...]
    moe = (info[:, _R_W1:_R_W1 + 1] * _from_slab(buf_ref, (2 * slot,), tc, slab)
           + info[:, _R_W2:_R_W2 + 1] * _from_slab(buf_ref, (2 * slot + 1,), tc, slab))
    o_ref[...] = _layer_norm(alpha * x_ref[...] + moe, lng_ref[...], lnb_ref[...])


def _combine(pos1, pos2, info, x2, ys, lng, lnb, alpha, tc):
    T, D = x2.shape
    n = T // tc
    cur = pl.BlockSpec((tc,), lambda i: (i,), memory_space=pltpu.SMEM)
    nxt = pl.BlockSpec((tc,), lambda i: (jnp.minimum(i + 1, n - 1),), memory_space=pltpu.SMEM)
    return pl.pallas_call(
        functools.partial(_combine_kernel, alpha=alpha),
        out_shape=jax.ShapeDtypeStruct((T, D), F32),
        grid=(n,),
        in_specs=[cur, cur, nxt, nxt,
                  pl.BlockSpec((tc, LANES), lambda i: (i, 0)),
                  pl.BlockSpec((tc, D), lambda i: (i, 0)),
                  pl.BlockSpec(memory_space=pl.ANY),
                  pl.BlockSpec((1, D), lambda i: (0, 0)),
                  pl.BlockSpec((1, D), lambda i: (0, 0))],
        out_specs=pl.BlockSpec((tc, D), lambda i: (i, 0)),
        scratch_shapes=[pltpu.VMEM((4, tc * (D // LANES), LANES), F32), pltpu.SemaphoreType.DMA((4,))],
        compiler_params=_cparams(("arbitrary",)),
        name="moe_combine",
    )(pos1, pos2, pos1, pos2, info, x2, ys, lng, lnb)


def _moe(x2, info, rows, counts, w_e_in, w_e_out, lng, lnb, alpha, tiles):
    T, D = x2.shape
    bm = tiles["expert"]
    cnt = counts[0, :N_EXPERTS].astype(I32)
    padded = ((cnt + bm - 1) // bm) * bm
    ends = jnp.cumsum(padded)
    offsets = ends - padded
    n_blocks = (2 * T) // bm + N_EXPERTS
    n_used = (ends[-1] // bm).astype(I32)
    blk_start = jnp.arange(n_blocks, dtype=I32) * bm
    block_expert = jnp.sum((blk_start[:, None] >= ends[None, :]).astype(I32), axis=1)
    last_expert = jnp.sum((blk_start[n_used - 1] >= ends).astype(I32))
    block_expert = jnp.where(jnp.arange(n_blocks) < n_used, block_expert, last_expert).astype(I32)

    pos = _positions(offsets.astype(I32), rows, tiles["pos"])
    pos1, pos2 = pos[0], pos[1]
    xs = _dispatch(pos1, pos2, x2, n_blocks * bm, tiles["dispatch"])
    ys = _experts(block_expert, n_used.reshape(1), xs, w_e_in, w_e_out, bm)
    return _combine(pos1, pos2, info, x2, ys, lng, lnb, alpha, tiles["combine"])


def _tiles(B, S):
    T = B * S
    pick = lambda want, n: math.gcd(want, n)
    return dict(mixer=pick(MIXER_ROWS, S), fox=pick(FOX_BLOCK, S), ret=pick(RET_ROWS, S), hgrn=pick(HGRN_ROWS, S),
                merge=pick(MERGE_ROWS, T),
                pos=pick(POS_LANES, T), dispatch=pick(DISPATCH_ROWS, T), expert=pick(EXPERT_ROWS, T),
                combine=pick(COMBINE_ROWS, T))


def _pack_input_weights(w_in, b_in):
    c = 3 * BRANCH_W
    n_mix = c + N_HEADS + 9 * BRANCH_W
    col_scale = jnp.ones((w_in.shape[-1],), F32).at[:BRANCH_W].set(HEAD_DIM ** -0.5 * LOG2E)
    wt = jnp.transpose(w_in * col_scale, (2, 0, 1))
    b = b_in * col_scale
    packed = []
    for l in range(w_in.shape[0]):
        w = wt[:, l, :]
        wt_mix = jnp.concatenate([w[:c + N_HEADS].astype(BF16), jnp.zeros((LANES - N_HEADS, w.shape[1]), BF16),
                                  w[c + N_HEADS:n_mix].astype(BF16)], axis=0)
        b_mix = jnp.concatenate([b[l, :c + N_HEADS], jnp.zeros((LANES - N_HEADS,), F32), b[l, c + N_HEADS:n_mix]])
        packed.append((wt_mix, b_mix[None, :], w[n_mix:].astype(BF16), b[l][None, n_mix:]))
    return packed


def kernel(x, mem, w_in, b_in, w_mem_kv, hgrn_lb, hgrn_norm_g, w_branch, w_out, ln_g, ln_b, w_router, b_router,
           w_e_in, w_e_out):
    B, S, D = x.shape
    T = B * S
    depth = w_in.shape[0]
    alpha = (2.0 * depth) ** 0.25
    tiles = _tiles(B, S)

    lb_all = jax.nn.softmax(hgrn_lb.astype(F32), axis=0)
    lb_all = jnp.cumsum(lb_all, axis=0) - lb_all[0:1]
    wr_pad = jnp.concatenate([w_router, jnp.zeros((D, LANES - N_EXPERTS), F32)], axis=1)
    br_pad = jnp.concatenate([b_router, jnp.full((LANES - N_EXPERTS,), MASK_VALUE, F32)])[None, :]

    x2 = x.reshape(T, D)
    packed = _pack_input_weights(w_in, b_in)
    for l in range(depth):
        wt_mix, b_mix, wt_gate, b_gate = packed[l]
        qkv, kb, o_ret, o_hg, o_mem = _mixer(x2.reshape(B, S, D), wt_mix, b_mix, mem, w_mem_kv[l].astype(BF16),
                                             lb_all[l][None, :], jnp.tile(hgrn_norm_g[l], N_HEADS)[None, :],
                                             tiles["mixer"], tiles["ret"], tiles["hgrn"])
        o_fox = _fox_attention(qkv, kb, tiles["fox"])
        branches = [o.reshape(T, BRANCH_W) for o in (o_fox, o_ret, o_hg, o_mem)]
        x2, info, rows, counts = _merge(x2, branches, wt_gate, b_gate, w_branch[l].astype(BF16),
                                        w_out[l].astype(BF16), ln_g[l, 0][None, :], ln_b[l, 0][None, :],
                                        wr_pad, br_pad, alpha, tiles["merge"])
        x2 = _moe(x2, info, rows, counts, w_e_in[l].astype(BF16), w_e_out[l].astype(BF16),
                  ln_g[l, 1][None, :], ln_b[l, 1][None, :], alpha, tiles)
    return x2.reshape(B, S, D)
```

```python
import functools
import math

import jax
import jax.numpy as jnp
from jax import lax
from jax.experimental import pallas as pl
from jax.experimental.pallas import tpu as pltpu

F32 = jnp.float32
BF16 = jnp.bfloat16
I32 = jnp.int32
HIGHEST = lax.Precision.HIGHEST

N_HEADS = 4
HEAD_DIM = 64
BRANCH_W = N_HEADS * HEAD_DIM
N_BRANCH = 4
RET_CHUNK = 64
ROPE_BASE = 10000.0
N_EXPERTS = 16
EXPERTS_PER_GROUP = 4
LN_EPS = 1e-5
GN_EPS = 1e-6
MASK_VALUE = -1e30
LOG2E = math.log2(math.e)

LANES = 128
V7X_VMEM_LIMIT_BYTES = 52 * 1024 * 1024

MIXER_ROWS = 512
FOX_BLOCK = 512
RET_ROWS = 256
HGRN_ROWS = 128
HGRN_SUB = 4
MERGE_ROWS = 512
POS_LANES = 2048
DISPATCH_ROWS = 512
EXPERT_ROWS = 512
COMBINE_ROWS = 256
DMA_ISSUE_UNROLL = 8

_W_FOX = 3 * BRANCH_W
_C_FOXF = _W_FOX
_C_RET = _C_FOXF + LANES
_C_HGF = _C_RET + 4 * BRANCH_W
_C_HGR = _C_HGF + BRANCH_W
_C_MQ = _C_HGR + 3 * BRANCH_W
_C_END = _C_MQ + BRANCH_W

_NT = (((1,), (1,)), ((), ()))


def _cparams(sem):
    return pltpu.CompilerParams(dimension_semantics=sem, vmem_limit_bytes=V7X_VMEM_LIMIT_BYTES)


def _head_id(shape):
    return lax.broadcasted_iota(I32, shape, len(shape) - 1) // HEAD_DIM


def _dot(a, b):
    return jnp.dot(a, b, preferred_element_type=F32)


def _dot_nt(a, b):
    return lax.dot_general(a, b, _NT, preferred_element_type=F32)


def _silu(x):
    return x * jax.nn.sigmoid(x)


def _layer_norm(z, g, b):
    mu = jnp.mean(z, axis=-1, keepdims=True)
    d = z - mu
    var = jnp.mean(d * d, axis=-1, keepdims=True)
    return d * lax.rsqrt(var + LN_EPS) * g + b


_FOX_BIAS_PARTS = 3


def _fox_spare_lane(h):
    return HEAD_DIM * ((h + 1) % N_HEADS)


def _fox_place_table():
    lane = jnp.arange(BRANCH_W)
    head = jnp.arange(LANES)
    dest = jnp.where(head < N_HEADS, HEAD_DIM * ((head + 1) % N_HEADS), -BRANCH_W)
    return jnp.stack([(lane[None, :] == dest[:, None] + part) for part in range(_FOX_BIAS_PARTS)]).astype(BF16)


def _key_bias_block(ff, tri, place_ref, carry_ref):
    ls = jax.nn.log_sigmoid(ff) * LOG2E
    cs = jnp.dot(tri, ls, precision=HIGHEST, preferred_element_type=F32) + carry_ref[0:1, :]
    carry_ref[...] = jnp.broadcast_to(cs[ff.shape[0] - 1:, :], carry_ref.shape)
    rest = -cs
    kb = jnp.zeros((ff.shape[0], BRANCH_W), F32)
    for part in range(_FOX_BIAS_PARTS):
        piece = rest.astype(BF16)
        rest = rest - piece.astype(F32)
        kb = kb + _dot(piece, place_ref[part])
    return kb.astype(BF16)


def _fox_lane_table():
    lane = jnp.arange(BRANCH_W)
    rows = []
    for h in range(N_HEADS):
        spare = _fox_spare_lane(h)
        bias = (lane >= spare) & (lane < spare + _FOX_BIAS_PARTS)
        rows += [lane // HEAD_DIM == h, bias, ~bias, lane == spare]
    return jnp.stack(rows).astype(F32)


def _fox_kernel(q_ref, k_ref, v_ref, kb_ref, lanes_ref, o_ref, qh_ref, m_ref, acc_ref):
    qi = pl.program_id(1)
    ki = pl.program_id(2)
    blk = q_ref.shape[1]

    def pattern(h, r):
        return lanes_ref[4 * h + r:4 * h + r + 1, :].astype(BF16)

    @pl.when(ki == 0)
    def _():
        m_ref[...] = jnp.full(m_ref.shape, MASK_VALUE, F32)
        acc_ref[...] = jnp.zeros(acc_ref.shape, F32)
        q = q_ref[0]
        for h in range(N_HEADS):
            qh_ref[h] = q * pattern(h, 0) + pattern(h, 1)

    def sweep(masked):
        k = k_ref[0]
        v = v_ref[0]
        kb = kb_ref[0]
        if masked:
            causal = lax.broadcasted_iota(I32, (blk, blk), 0) >= lax.broadcasted_iota(I32, (blk, blk), 1)
        def logits(h):
            return _dot_nt(qh_ref[h], k * pattern(h, 2) + kb * pattern(h, 1))

        def softmax_step(h, s):
            if masked:
                s = jnp.where(causal, s, MASK_VALUE)
            m_prev = m_ref[h]
            m_new = jnp.maximum(m_prev, jnp.max(s, axis=-1, keepdims=True))
            m_ref[h] = m_new
            p = jnp.exp2(s - jnp.concatenate([m_new] * (blk // LANES), axis=1))
            return p.astype(BF16), jnp.exp2(m_prev - m_new)

        def accumulate(h, p, alpha):
            ones_lane = pattern(h, 3)
            vh = v * (1 - ones_lane) + ones_lane
            acc_ref[h] = acc_ref[h] * jnp.concatenate([alpha] * (BRANCH_W // LANES), axis=1) + _dot(p, vh)

        s_next = logits(0)
        for h in range(N_HEADS):
            s = s_next
            if h + 1 < N_HEADS:
                s_next = logits(h + 1)
            p, alpha = softmax_step(h, s)
            accumulate(h, p, alpha)

    @pl.when(ki < qi)
    def _():
        sweep(False)

    @pl.when(ki == qi)
    def _():
        sweep(True)
        lane = lax.broadcasted_iota(I32, (blk, BRANCH_W), 1)
        out = jnp.zeros((blk, BRANCH_W), F32)
        for h in range(N_HEADS):
            acc = acc_ref[h]
            spare = _fox_spare_lane(h)
            out = jnp.where((lane // HEAD_DIM) == h, acc / acc[:, spare:spare + 1], out)
        o_ref[0] = out.astype(BF16)


def _fox_attention(qkv3, kb, blk):
    B, S, _ = qkv3.shape
    n = S // blk
    kv_blk = lambda c: pl.BlockSpec((1, blk, BRANCH_W), lambda b, qi, ki: (b, jnp.minimum(ki, qi), c))
    lanes = _fox_lane_table()
    return pl.pallas_call(
        _fox_kernel,
        out_shape=jax.ShapeDtypeStruct((B, S, BRANCH_W), BF16),
        grid=(B, n, n),
        in_specs=[pl.BlockSpec((1, blk, BRANCH_W), lambda b, qi, ki: (b, qi, 0)), kv_blk(1), kv_blk(2),
                  pl.BlockSpec((1, blk, BRANCH_W), lambda b, qi, ki: (b, jnp.minimum(ki, qi), 0)),
                  pl.BlockSpec(lanes.shape, lambda b, qi, ki: (0, 0))],
        out_specs=pl.BlockSpec((1, blk, BRANCH_W), lambda b, qi, ki: (b, qi, 0)),
        scratch_shapes=[pltpu.VMEM((N_HEADS, blk, BRANCH_W), BF16),
                        pltpu.VMEM((N_HEADS, blk, LANES), F32),
                        pltpu.VMEM((N_HEADS, blk, BRANCH_W), F32)],
        compiler_params=_cparams(("parallel", "parallel", "arbitrary")),
        name="fox_attention",
    )(qkv3, qkv3, qkv3, kb, lanes)


def _mem_block(q, mk, mv):
    rows = q.shape[0]
    q = q * (HEAD_DIM ** -0.5)
    hid = _head_id((rows, BRANCH_W))
    out = jnp.zeros((rows, BRANCH_W), F32)
    for h in range(N_HEADS):
        qh = jnp.where(hid == h, q, jnp.zeros_like(q))
        s = _dot_nt(qh, mk)
        p = jnp.exp(s - jnp.max(s, axis=-1, keepdims=True))
        pv = _dot(p.astype(BF16), mv)
        out = jnp.where(hid == h, pv / jnp.sum(p, axis=-1, keepdims=True), out)
    return out.astype(BF16)


def _block_diag(val_per_head):
    hid = jnp.arange(BRANCH_W) // HEAD_DIM
    same = hid[:, None] == hid[None, :]
    return jnp.where(same, jnp.asarray(val_per_head, F32)[hid][:, None], 0.0)


def _rope_tables(S):
    half = HEAD_DIM // 2
    inv = ROPE_BASE ** (-2.0 * jnp.arange(half, dtype=F32) / HEAD_DIM)
    ang = jnp.arange(S, dtype=F32)[:, None] * inv[None, :]
    cos = jnp.tile(jnp.cos(ang), (1, 2 * N_HEADS))
    sin = jnp.tile(jnp.sin(ang), (1, 2 * N_HEADS))
    d = jnp.arange(BRANCH_W)
    lo = (d % HEAD_DIM) < half
    perm = jnp.where(lo[None, :] & (d[:, None] == d[None, :] + half), -1.0, 0.0)
    perm = perm + jnp.where((~lo)[None, :] & (d[:, None] == d[None, :] - half), 1.0, 0.0)
    return cos, sin, perm.astype(BF16)


def _ret_tables(R):
    log_gamma = jnp.log1p(-jnp.exp2(-5.0 - jnp.arange(N_HEADS, dtype=F32)))
    idx = jnp.arange(R, dtype=F32)
    chunk = jnp.arange(R) // RET_CHUNK
    dist = jnp.abs(idx[:, None] - idx[None, :])
    visible = chunk[None, :] <= chunk[:, None]
    decay = jnp.where(visible[None], jnp.exp(log_gamma[:, None, None] * dist[None]), 0.0)
    lanes_lg = jnp.repeat(log_gamma, HEAD_DIM)[None, :]
    qdec = jnp.exp(lanes_lg * (idx[:, None] + 1.0))
    kdec = jnp.exp(lanes_lg * (R - 1.0 - idx[:, None]))
    sdec = _block_diag(jnp.exp(log_gamma * R))
    return decay, qdec, kdec, sdec


def _ret_block(qb, kb, v, g, cos, sin, perm, decay_ref, qdec, kdec, sdec, bd, gmean, st_ref):
    R = qb.shape[0]
    qr = qb.astype(F32) * cos + _dot(qb, perm) * sin
    kr = (kb.astype(F32) * cos + _dot(kb, perm) * sin) * (HEAD_DIM ** -0.5)
    qrb = qr.astype(BF16)
    krb = kr.astype(BF16)
    hid = _head_id((R, BRANCH_W))

    st = st_ref[...]
    o = _dot((qr * qdec).astype(BF16), st.astype(BF16))
    for h in range(N_HEADS):
        qh = jnp.where(hid == h, qrb, jnp.zeros_like(qrb))
        a = _dot_nt(qh, krb) * decay_ref[h]
        o = o + jnp.where(hid == h, _dot(a.astype(BF16), v), 0.0)
    kd_t = (kr * kdec).T.astype(BF16)
    st_ref[...] = st * sdec + _dot(kd_t, v) * bd

    mu = _dot(o.astype(BF16), gmean)
    d = o - mu
    var = _dot((d * d).astype(BF16), gmean)
    on = d * lax.rsqrt(var + GN_EPS)
    return (on * _silu(g.astype(F32))).astype(BF16)


def _hgrn_levels(R):
    levels = []
    m = HGRN_SUB
    while 2 * m <= R:
        levels.append(m)
        m *= 2
    return tuple(levels)


def _hgrn_block(zf, vb, qb, g, lb, ng, tri, seg_ref, bd, gsum, gmean, st_ref):
    R = zf.shape[0]
    W = BRANCH_W
    kf = (1.0 - lb) * jax.nn.sigmoid(-zf)
    logf = jnp.log1p(-kf)
    b = jnp.dot(tri, logf, precision=HIGHEST, preferred_element_type=F32)
    q = qb.astype(F32)
    v = vb.astype(F32)
    hid = _head_id((R, W))
    row = lax.broadcasted_iota(I32, (R, W), 0)

    o = _dot((q * kf).astype(BF16), gsum) * v
    sub = row % HGRN_SUB
    for d in range(1, HGRN_SUB):
        e = jnp.exp(b - pltpu.roll(b, d, 0))
        w = jnp.where(sub >= d, q * pltpu.roll(kf, d, 0) * e, 0.0)
        o = o + _dot(w.astype(BF16), gsum) * pltpu.roll(v, d, 0)

    a_heads = [jnp.zeros((R, R), F32) for _ in range(N_HEADS)]
    for li, m in enumerate(_hgrn_levels(R)):
        nseg = R // (2 * m)
        bnd = jnp.broadcast_to(b.reshape(nseg, 2 * m, W)[:, m - 1:m, :], (nseg, 2 * m, W)).reshape(R, W)
        second = (row % (2 * m)) >= m
        qm = jnp.where(second, q * jnp.exp(b - bnd), 0.0).astype(BF16)
        km = jnp.where(second, 0.0, kf * jnp.exp(bnd - b)).astype(BF16)
        for h in range(N_HEADS):
            qh = jnp.where(hid == h, qm, jnp.zeros_like(qm))
            pairs = _dot_nt(qh, km)
            a_heads[h] = a_heads[h] + (pairs if nseg == 1 else pairs * seg_ref[li])
    for h in range(N_HEADS):
        o = o + jnp.where(hid == h, _dot(a_heads[h].astype(BF16), vb), 0.0)

    st = st_ref[...]
    o = o + _dot_nt((q * jnp.exp(b)).astype(BF16), st.astype(BF16))
    b_last = b[R - 1:R, :]
    kd = (kf * jnp.exp(b_last - b)).astype(BF16)
    st_ref[...] = st * jnp.exp(b_last) + _dot(v.T.astype(BF16), kd) * bd

    ms = _dot((o * o).astype(BF16), gmean)
    on = o * lax.rsqrt(ms + GN_EPS) * ng
    return (on * _silu(g.astype(F32))).astype(BF16)


def _mixer_kernel(x_ref, wt_ref, b_ref, mem_ref, wkv_ref, cos_ref, sin_ref, perm_ref, decay_ref, qdec_ref,
                  kdec_ref, sdec_ref, bd_ref, gsum_ref, gmean_ref, lb_ref, ng_ref, tri_ref, seg_ref, place_ref,
                  qkv_ref, kb_ref, oret_ref, ohg_ref, omem_ref,
                  kv_ref, ret_st_ref, hg_st_ref, carry_ref):
    @pl.when(pl.program_id(1) == 0)
    def _():
        kv_ref[...] = _dot(mem_ref[0].astype(BF16), wkv_ref[...]).astype(BF16)
        ret_st_ref[...] = jnp.zeros(ret_st_ref.shape, F32)
        hg_st_ref[...] = jnp.zeros(hg_st_ref.shape, F32)
        carry_ref[...] = jnp.zeros(carry_ref.shape, F32)

    xb = x_ref[0].astype(BF16)
    tm = xb.shape[0]
    W = BRANCH_W

    def seg(lo, hi):
        return _dot_nt(xb, wt_ref[lo:hi, :]) + b_ref[:, lo:hi]

    hgf = seg(_C_HGF, _C_HGR)
    hgr = seg(_C_HGR, _C_MQ).astype(BF16)
    bd = bd_ref[...]
    gmean = gmean_ref[...]
    tri = tri_ref[...]
    R = tri.shape[0]
    for r in range(tm // R):
        rows = slice(r * R, (r + 1) * R)
        ohg_ref[0, rows, :] = _hgrn_block(hgf[rows], hgr[rows, :W], hgr[rows, W:2 * W], hgr[rows, 2 * W:],
                                          lb_ref[...], ng_ref[...], tri, seg_ref, bd, gsum_ref[...], gmean, hg_st_ref)

    ff = seg(_C_FOXF, _C_RET)
    for r in range(tm // R):
        rows = slice(r * R, (r + 1) * R)
        kb_ref[0, rows, :] = _key_bias_block(ff[rows], tri, place_ref, carry_ref)
    qkv_ref[0] = seg(0, _C_FOXF).astype(BF16)

    ret = seg(_C_RET, _C_HGF).astype(BF16)
    RR = qdec_ref.shape[0]
    for r in range(tm // RR):
        rows = slice(r * RR, (r + 1) * RR)
        oret_ref[0, rows, :] = _ret_block(ret[rows, :W], ret[rows, W:2 * W], ret[rows, 2 * W:3 * W], ret[rows, 3 * W:],
                                          cos_ref[rows, :], sin_ref[rows, :], perm_ref[...], decay_ref, qdec_ref[...],
                                          kdec_ref[...], sdec_ref[...], bd, gmean, ret_st_ref)

    mq = seg(_C_MQ, _C_END).astype(BF16)
    omem_ref[0] = _mem_block(mq, kv_ref[:, :W], kv_ref[:, W:])


def _mixer(x3, wt_mix, b_mix, mem, w_kv, lb_row, ng_row, tm, ret_rows, hgrn_rows):
    B, S, D = x3.shape
    M = mem.shape[1]
    cos, sin, perm = _rope_tables(S)
    decay, qdec, kdec, sdec = _ret_tables(ret_rows)
    bd = _block_diag(jnp.ones((N_HEADS,), F32))
    gsum = bd.astype(BF16)
    gmean = (bd / HEAD_DIM).astype(BF16)
    idx = jnp.arange(hgrn_rows)
    tri = (idx[:, None] >= idx[None, :]).astype(F32)
    seg = jnp.stack([(idx[:, None] // (2 * m) == idx[None, :] // (2 * m)).astype(F32)
                     for m in _hgrn_levels(hgrn_rows)])
    place = _fox_place_table()
    const = lambda a: pl.BlockSpec(a.shape, lambda b, j: (0,) * a.ndim)
    tile = lambda w: pl.BlockSpec((1, tm, w), lambda b, j: (b, j, 0))
    pos_tab = pl.BlockSpec((tm, BRANCH_W), lambda b, j: (j, 0))
    return pl.pallas_call(
        _mixer_kernel,
        out_shape=[jax.ShapeDtypeStruct((B, S, _C_FOXF), BF16)] + [jax.ShapeDtypeStruct((B, S, BRANCH_W), BF16)] * 4,
        grid=(B, S // tm),
        in_specs=[tile(D), const(wt_mix), const(b_mix),
                  pl.BlockSpec((1, M, D), lambda b, j: (b, 0, 0)), const(w_kv),
                  pos_tab, pos_tab, const(perm), const(decay), const(qdec), const(kdec), const(sdec),
                  const(bd), const(gsum), const(gmean), const(lb_row), const(ng_row), const(tri), const(seg),
                  const(place)],
        out_specs=[tile(_C_FOXF)] + [tile(BRANCH_W)] * 4,
        scratch_shapes=[pltpu.VMEM((M, 2 * BRANCH_W), BF16),
                        pltpu.VMEM((BRANCH_W, BRANCH_W), F32),
                        pltpu.VMEM((BRANCH_W, BRANCH_W), F32),
                        pltpu.VMEM((8, LANES), F32)],
        compiler_params=_cparams(("parallel", "arbitrary")),
        name="mixer",
    )(x3, wt_mix, b_mix, mem, w_kv, cos, sin, perm, decay, qdec, kdec, sdec, bd, gsum, gmean, lb_row, ng_row,
      tri, seg, place)


_R_W1, _R_W2, _R_E1, _R_E2, _R_RANK1, _R_RANK2 = range(6)


def _route_block(x1, wr_hi_ref, wr_lo_ref, br_ref, lstrict_ref, run_ref):
    tm = x1.shape[0]
    x_hi = x1.astype(BF16)
    x_lo = (x1 - x_hi.astype(F32)).astype(BF16)
    logits = (_dot(x_hi, wr_hi_ref[...]) + _dot(x_hi, wr_lo_ref[...]) + _dot(x_lo, wr_hi_ref[...])) + br_ref[...]
    lane = lax.broadcasted_iota(I32, (tm, LANES), 1).astype(F32)
    group = jnp.floor(lane * (1.0 / EXPERTS_PER_GROUP))
    e = jnp.exp(logits - jnp.max(logits, axis=-1, keepdims=True))
    p = e / jnp.sum(e, axis=-1, keepdims=True)
    p1 = jnp.max(p, axis=-1, keepdims=True)
    e1 = jnp.min(jnp.where(p == p1, lane, float(LANES)), axis=-1, keepdims=True)
    in_group = group == jnp.floor(e1 * (1.0 / EXPERTS_PER_GROUP))
    rest = jnp.where(in_group, jnp.where(lane == e1, -1.0, p), -1.0)
    p2 = jnp.max(rest, axis=-1, keepdims=True)
    e2 = jnp.min(jnp.where(rest == p2, lane, float(LANES)), axis=-1, keepdims=True)
    w1 = p1 / (p1 + p2)
    w2 = p2 / (p1 + p2)

    sel = jnp.where(lane == e1, 1.0, jnp.where(lane == e2, 1.0, 0.0))
    before = _dot(lstrict_ref[...], sel.astype(BF16)) + run_ref[0:1, :]
    rank1 = jnp.sum(jnp.where(lane == e1, before, 0.0), axis=-1, keepdims=True)
    rank2 = jnp.sum(jnp.where(lane == e2, before, 0.0), axis=-1, keepdims=True)
    run_ref[...] = run_ref[...] + jnp.sum(sel, axis=0, keepdims=True)

    info = jnp.zeros((tm, LANES), F32)
    for col, val in ((_R_W1, w1), (_R_W2, w2), (_R_E1, e1), (_R_E2, e2), (_R_RANK1, rank1), (_R_RANK2, rank2)):
        info = jnp.where(lane == float(col), val, info)
    pick = (lax.broadcasted_iota(I32, (8, LANES), 0) == lax.broadcasted_iota(I32, (8, LANES), 1)).astype(F32)
    rows = lax.dot_general(pick, info, _NT, precision=HIGHEST, preferred_element_type=F32)
    return info, rows


def _merge_kernel(x_ref, b0_ref, b1_ref, b2_ref, b3_ref, wgt_ref, bg_ref, wb_ref, wo_ref, lng_ref, lnb_ref,
                  wr_hi_ref, wr_lo_ref, br_ref, lstrict_ref, o_ref, info_ref, rows_ref, cnt_ref, run_ref, *, alpha):
    @pl.when(pl.program_id(0) == 0)
    def _():
        run_ref[...] = jnp.zeros(run_ref.shape, F32)

    x = x_ref[...]
    xb = x.astype(BF16)
    D = x.shape[1]
    mixed = jnp.zeros(x.shape, F32)
    for n, br in enumerate((b0_ref, b1_ref, b2_ref, b3_ref)):
        gate = jax.nn.sigmoid(_dot_nt(xb, wgt_ref[n * D:(n + 1) * D, :]) + bg_ref[:, n * D:(n + 1) * D])
        mixed = mixed + gate * _dot(br[...], wb_ref[n])
    z = alpha * x + _dot(mixed.astype(BF16), wo_ref[...])
    x1 = _layer_norm(z, lng_ref[...], lnb_ref[...])
    o_ref[...] = x1
    info, rows = _route_block(x1, wr_hi_ref, wr_lo_ref, br_ref, lstrict_ref, run_ref)
    info_ref[...] = info
    rows_ref[...] = rows
    cnt_ref[...] = run_ref[...]


def _merge(x2, branches, wgt, bg, wb, wo, lng, lnb, wr_pad, br_pad, alpha, tm):
    T, D = x2.shape
    const = lambda a: pl.BlockSpec(a.shape, lambda i: (0,) * a.ndim)
    br_spec = pl.BlockSpec((tm, BRANCH_W), lambda i: (i, 0))
    idx = jnp.arange(tm)
    lstrict = (idx[:, None] > idx[None, :]).astype(BF16)
    wr_hi = wr_pad.astype(BF16)
    wr_lo = (wr_pad - wr_hi.astype(F32)).astype(BF16)
    return pl.pallas_call(
        functools.partial(_merge_kernel, alpha=alpha),
        out_shape=[jax.ShapeDtypeStruct((T, D), F32), jax.ShapeDtypeStruct((T, LANES), F32),
                   jax.ShapeDtypeStruct((8, T), F32), jax.ShapeDtypeStruct((8, LANES), F32)],
        grid=(T // tm,),
        in_specs=[pl.BlockSpec((tm, D), lambda i: (i, 0)), br_spec, br_spec, br_spec, br_spec,
                  const(wgt), const(bg), const(wb), const(wo), const(lng), const(lnb),
                  const(wr_hi), const(wr_lo), const(br_pad), const(lstrict)],
        out_specs=[pl.BlockSpec((tm, D), lambda i: (i, 0)),
                   pl.BlockSpec((tm, LANES), lambda i: (i, 0)),
                   pl.BlockSpec((8, tm), lambda i: (0, i)),
                   pl.BlockSpec((8, LANES), lambda i: (0, 0))],
        scratch_shapes=[pltpu.VMEM((8, LANES), F32)],
        compiler_params=_cparams(("arbitrary",)),
        name="merge_ln_route",
    )(x2, *branches, wgt, bg, wb, wo, lng, lnb, wr_hi, wr_lo, br_pad, lstrict)


def _pos_kernel(off_ref, rows_ref, pos_ref):
    rows = rows_ref[...]
    e = rows[_R_E1:_R_E2 + 1, :]
    start = jnp.zeros(e.shape, F32)
    for ex in range(N_EXPERTS):
        start = jnp.where(e == float(ex), off_ref[ex].astype(F32), start)
    pos = (rows[_R_RANK1:_R_RANK2 + 1, :] + start).astype(I32)
    pos_ref[...] = jnp.zeros(pos_ref.shape, I32)
    pos_ref[0:2, :] = pos


def _positions(offsets, rows, tl):
    T = rows.shape[1]
    return pl.pallas_call(
        _pos_kernel,
        out_shape=jax.ShapeDtypeStruct((8, T), I32),
        grid_spec=pltpu.PrefetchScalarGridSpec(
            num_scalar_prefetch=1, grid=(T // tl,),
            in_specs=[pl.BlockSpec((8, tl), lambda i, off: (0, i))],
            out_specs=pl.BlockSpec((8, tl), lambda i, off: (0, i))),
        compiler_params=_cparams(("parallel",)),
        name="moe_positions",
    )(offsets, rows)


def _from_slab(ref, idx, rows, slab):
    return jnp.concatenate([ref[idx + (pl.ds(s, rows, stride=slab), slice(None))] for s in range(slab)], axis=1)


def _dispatch_kernel(pad_lo_ref, pad_hi_ref, nu_ref, pos1_ref, pos2_ref, x_ref, xs_ref, slab_ref, zero_ref, sem,
                     *, bm):
    td, D = x_ref.shape
    slab = D // LANES
    i = pl.program_id(0)
    slot = i % 2

    def wait_slot(s):
        for _ in range(2):
            pltpu.make_async_copy(slab_ref.at[s], xs_ref.at[pl.ds(0, td * slab)], sem.at[s]).wait()

    @pl.when(i >= 2)
    def _():
        wait_slot(slot)

    x = x_ref[...]
    for s in range(slab):
        slab_ref[slot, pl.ds(s, td, stride=slab), :] = x[:, s * LANES:(s + 1) * LANES]

    def issue(t, carry):
        src = slab_ref.at[slot, pl.ds(pl.multiple_of(t * slab, slab), slab)]
        for pos_ref in (pos1_ref, pos2_ref):
            dst = xs_ref.at[pl.ds(pl.multiple_of(pos_ref[t] * slab, slab), slab)]
            pltpu.make_async_copy(src, dst, sem.at[slot]).start()
        return carry

    lax.fori_loop(0, td, issue, 0, unroll=DMA_ISSUE_UNROLL)

    last = pl.num_programs(0) - 1

    @pl.when(i == last)
    def _():
        wait_slot(slot)

    @pl.when(jnp.logical_and(i == last, i >= 1))
    def _():
        wait_slot(1 - slot)

    @pl.when(i == last)
    def _():
        zero_ref[...] = jnp.zeros(zero_ref.shape, F32)
        zero_row = zero_ref.at[pl.ds(0, slab)]
        n_blocks = xs_ref.shape[0] // (bm * slab)

        def row_copy(r):
            return pltpu.make_async_copy(zero_row, xs_ref.at[pl.ds(pl.multiple_of(r * slab, slab), slab)], sem.at[2])

        def block_copy(blk):
            start = pl.multiple_of(blk * (bm * slab), bm * slab)
            return pltpu.make_async_copy(zero_ref, xs_ref.at[pl.ds(start, bm * slab)], sem.at[2])

        def start_row(r, carry):
            row_copy(r).start()
            return carry

        def wait_row(r, carry):
            row_copy(r).wait()
            return carry

        def start_block(blk, carry):
            block_copy(blk).start()
            return carry

        def wait_block(blk, carry):
            block_copy(blk).wait()
            return carry

        for e in range(N_EXPERTS):
            lax.fori_loop(pad_lo_ref[e], pad_hi_ref[e], start_row, 0)
        lax.fori_loop(nu_ref[0], n_blocks, start_block, 0)
        for e in range(N_EXPERTS):
            lax.fori_loop(pad_lo_ref[e], pad_hi_ref[e], wait_row, 0)
        lax.fori_loop(nu_ref[0], n_blocks, wait_block, 0)


def _dispatch(pad_lo, pad_hi, n_used, pos1, pos2, x2, n_slots, td, bm):
    T, D = x2.shape
    slab = D // LANES
    smem_blk = pl.BlockSpec((td,), lambda i, lo, hi, nu: (i,), memory_space=pltpu.SMEM)
    return pl.pallas_call(
        functools.partial(_dispatch_kernel, bm=bm),
        out_shape=jax.ShapeDtypeStruct((n_slots * slab, LANES), F32),
        grid_spec=pltpu.PrefetchScalarGridSpec(
            num_scalar_prefetch=3, grid=(T // td,),
            in_specs=[smem_blk, smem_blk, pl.BlockSpec((td, D), lambda i, lo, hi, nu: (i, 0))],
            out_specs=pl.BlockSpec(memory_space=pl.ANY),
            scratch_shapes=[pltpu.VMEM((2, td * slab, LANES), F32), pltpu.VMEM((bm * slab, LANES), F32),
                            pltpu.SemaphoreType.DMA((3,))]),
        compiler_params=_cparams(("arbitrary",)),
        name="moe_dispatch",
    )(pad_lo, pad_hi, n_used, pos1, pos2, x2)


def _expert_kernel(be_ref, nu_ref, xs_ref, wi_ref, wo_ref, ys_ref):
    del be_ref
    used = pl.program_id(0) < nu_ref[0]

    @pl.when(used)
    def _():
        F = wo_ref.shape[1]
        slab = wo_ref.shape[2] // LANES
        bm = xs_ref.shape[0] // slab
        x = _from_slab(xs_ref, (), bm, slab)
        h = _dot(x.astype(BF16), wi_ref[0])
        act = _silu(h[:, :F]) * h[:, F:]
        y = _dot(act.astype(BF16), wo_ref[0])
        for s in range(slab):
            ys_ref[pl.ds(s, bm, stride=slab), :] = y[:, s * LANES:(s + 1) * LANES]

    @pl.when(jnp.logical_not(used))
    def _():
        ys_ref[...] = jnp.zeros(ys_ref.shape, F32)


def _experts(block_expert, n_used, xs, w_in, w_out, bm):
    E, D, F2 = w_in.shape
    slab = D // LANES
    n_blocks = xs.shape[0] // (bm * slab)
    row_blk = lambda i, be, nu: (jnp.maximum(jnp.minimum(i, nu[0] - 1), 0), 0)
    return pl.pallas_call(
        _expert_kernel,
        out_shape=jax.ShapeDtypeStruct(xs.shape, F32),
        grid_spec=pltpu.PrefetchScalarGridSpec(
            num_scalar_prefetch=2, grid=(n_blocks,),
            in_specs=[pl.BlockSpec((bm * slab, LANES), row_blk),
                      pl.BlockSpec((1, D, F2), lambda i, be, nu: (be[i], 0, 0)),
                      pl.BlockSpec((1, F2 // 2, D), lambda i, be, nu: (be[i], 0, 0))],
            out_specs=pl.BlockSpec((bm * slab, LANES), lambda i, be, nu: (i, 0))),
        compiler_params=_cparams(("arbitrary",)),
        name="moe_experts",
    )(block_expert, n_used, xs, w_in, w_out)


def _combine_kernel(pos1_ref, pos2_ref, nxt1_ref, nxt2_ref, info_ref, x_ref, ys_ref, lng_ref, lnb_ref, o_ref,
                    buf_ref, sem, *, alpha):
    tc, D = x_ref.shape
    slab = D // LANES
    i = pl.program_id(0)
    slot = i % 2

    def gather(p1_ref, p2_ref, into):
        def issue(t, carry):
            for k, pos_ref in enumerate((p1_ref, p2_ref)):
                src = ys_ref.at[pl.ds(pl.multiple_of(pos_ref[t] * slab, slab), slab)]
                dst = buf_ref.at[2 * into + k, pl.ds(pl.multiple_of(t * slab, slab), slab)]
                pltpu.make_async_copy(src, dst, sem.at[2 * into + k]).start()
            return carry

        lax.fori_loop(0, tc, issue, 0, unroll=DMA_ISSUE_UNROLL)

    @pl.when(i == 0)
    def _():
        gather(pos1_ref, pos2_ref, slot)

    @pl.when(i + 1 < pl.num_programs(0))
    def _():
        gather(nxt1_ref, nxt2_ref, 1 - slot)

    for k in range(2):
        pltpu.make_async_copy(ys_ref.at[pl.ds(0, tc * slab)], buf_ref.at[2 * slot + k], sem.at[2 * slot + k]).wait()
    info = info_ref[...]
    moe = (info[:, _R_W1:_R_W1 + 1] * _from_slab(buf_ref, (2 * slot,), tc, slab)
           + info[:, _R_W2:_R_W2 + 1] * _from_slab(buf_ref, (2 * slot + 1,), tc, slab))
    o_ref[...] = _layer_norm(alpha * x_ref[...] + moe, lng_ref[...], lnb_ref[...])


def _combine(pos1, pos2, info, x2, ys, lng, lnb, alpha, tc):
    T, D = x2.shape
    n = T // tc
    cur = pl.BlockSpec((tc,), lambda i: (i,), memory_space=pltpu.SMEM)
    nxt = pl.BlockSpec((tc,), lambda i: (jnp.minimum(i + 1, n - 1),), memory_space=pltpu.SMEM)
    return pl.pallas_call(
        functools.partial(_combine_kernel, alpha=alpha),
        out_shape=jax.ShapeDtypeStruct((T, D), F32),
        grid=(n,),
        in_specs=[cur, cur, nxt, nxt,
                  pl.BlockSpec((tc, LANES), lambda i: (i, 0)),
                  pl.BlockSpec((tc, D), lambda i: (i, 0)),
                  pl.BlockSpec(memory_space=pl.ANY),
                  pl.BlockSpec((1, D), lambda i: (0, 0)),
                  pl.BlockSpec((1, D), lambda i: (0, 0))],
        out_specs=pl.BlockSpec((tc, D), lambda i: (i, 0)),
        scratch_shapes=[pltpu.VMEM((4, tc * (D // LANES), LANES), F32), pltpu.SemaphoreType.DMA((4,))],
        compiler_params=_cparams(("arbitrary",)),
        name="moe_combine",
    )(pos1, pos2, pos1, pos2, info, x2, ys, lng, lnb)


def _moe(x2, info, rows, counts, w_e_in, w_e_out, lng, lnb, alpha, tiles):
    T, D = x2.shape
    bm = tiles["expert"]
    cnt = counts[0, :N_EXPERTS].astype(I32)
    padded = ((cnt + bm - 1) // bm) * bm
    ends = jnp.cumsum(padded)
    offsets = ends - padded
    n_blocks = (2 * T) // bm + N_EXPERTS
    n_used = (ends[-1] // bm).astype(I32)
    blk_start = jnp.arange(n_blocks, dtype=I32) * bm
    block_expert = jnp.sum((blk_start[:, None] >= ends[None, :]).astype(I32), axis=1)
    last_expert = jnp.sum((blk_start[n_used - 1] >= ends).astype(I32))
    block_expert = jnp.where(jnp.arange(n_blocks) < n_used, block_expert, last_expert).astype(I32)

    pos = _positions(offsets.astype(I32), rows, tiles["pos"])
    pos1, pos2 = pos[0], pos[1]
    n_used = n_used.reshape(1)
    xs = _dispatch((offsets + cnt).astype(I32), ends.astype(I32), n_used, pos1, pos2, x2, n_blocks * bm,
                   tiles["dispatch"], bm)
    ys = _experts(block_expert, n_used, xs, w_e_in, w_e_out, bm)
    return _combine(pos1, pos2, info, x2, ys, lng, lnb, alpha, tiles["combine"])


def _tiles(B, S):
    T = B * S
    pick = lambda want, n: math.gcd(want, n)
    return dict(mixer=pick(MIXER_ROWS, S), fox=pick(FOX_BLOCK, S), ret=pick(RET_ROWS, S), hgrn=pick(HGRN_ROWS, S),
                merge=pick(MERGE_ROWS, T),
                pos=pick(POS_LANES, T), dispatch=pick(DISPATCH_ROWS, T), expert=pick(EXPERT_ROWS, T),
                combine=pick(COMBINE_ROWS, T))


def _pack_input_weights(w_in, b_in):
    c = 3 * BRANCH_W
    n_mix = c + N_HEADS + 9 * BRANCH_W
    col_scale = jnp.ones((w_in.shape[-1],), F32).at[:BRANCH_W].set(HEAD_DIM ** -0.5 * LOG2E)
    wt = jnp.transpose(w_in * col_scale, (2, 0, 1))
    b = b_in * col_scale
    packed = []
    for l in range(w_in.shape[0]):
        w = wt[:, l, :]
        wt_mix = jnp.concatenate([w[:c + N_HEADS].astype(BF16), jnp.zeros((LANES - N_HEADS, w.shape[1]), BF16),
                                  w[c + N_HEADS:n_mix].astype(BF16)], axis=0)
        b_mix = jnp.concatenate([b[l, :c + N_HEADS], jnp.zeros((LANES - N_HEADS,), F32), b[l, c + N_HEADS:n_mix]])
        packed.append((wt_mix, b_mix[None, :], w[n_mix:].astype(BF16), b[l][None, n_mix:]))
    return packed


def kernel(x, mem, w_in, b_in, w_mem_kv, hgrn_lb, hgrn_norm_g, w_branch, w_out, ln_g, ln_b, w_router, b_router,
           w_e_in, w_e_out):
    B, S, D = x.shape
    T = B * S
    depth = w_in.shape[0]
    alpha = (2.0 * depth) ** 0.25
    tiles = _tiles(B, S)

    lb_all = jax.nn.softmax(hgrn_lb.astype(F32), axis=0)
    lb_all = jnp.cumsum(lb_all, axis=0) - lb_all[0:1]
    wr_pad = jnp.concatenate([w_router, jnp.zeros((D, LANES - N_EXPERTS), F32)], axis=1)
    br_pad = jnp.concatenate([b_router, jnp.full((LANES - N_EXPERTS,), MASK_VALUE, F32)])[None, :]

    x2 = x.reshape(T, D)
    packed = _pack_input_weights(w_in, b_in)
    for l in range(depth):
        wt_mix, b_mix, wt_gate, b_gate = packed[l]
        qkv, kb, o_ret, o_hg, o_mem = _mixer(x2.reshape(B, S, D), wt_mix, b_mix, mem, w_mem_kv[l].astype(BF16),
                                             lb_all[l][None, :], jnp.tile(hgrn_norm_g[l], N_HEADS)[None, :],
                                             tiles["mixer"], tiles["ret"], tiles["hgrn"])
        o_fox = _fox_attention(qkv, kb, tiles["fox"])
        branches = [o.reshape(T, BRANCH_W) for o in (o_fox, o_ret, o_hg, o_mem)]
        x2, info, rows, counts = _merge(x2, branches, wt_gate, b_gate, w_branch[l].astype(BF16),
                                        w_out[l].astype(BF16), ln_g[l, 0][None, :], ln_b[l, 0][None, :],
                                        wr_pad, br_pad, alpha, tiles["merge"])
        x2 = _moe(x2, info, rows, counts, w_e_in[l].astype(BF16), w_e_out[l].astype(BF16),
                  ln_g[l, 1][None, :], ln_b[l, 1][None, :], alpha, tiles)
    return x2.reshape(B, S, D)
```

```python
import functools
import math

import jax
import jax.numpy as jnp
from jax import lax
from jax.experimental import pallas as pl
from jax.experimental.pallas import tpu as pltpu

F32 = jnp.float32
BF16 = jnp.bfloat16
I32 = jnp.int32
HIGHEST = lax.Precision.HIGHEST

N_HEADS = 4
HEAD_DIM = 64
BRANCH_W = N_HEADS * HEAD_DIM
N_BRANCH = 4
RET_CHUNK = 64
ROPE_BASE = 10000.0
N_EXPERTS = 16
EXPERTS_PER_GROUP = 4
PAIRS_PER_GROUP = EXPERTS_PER_GROUP * (EXPERTS_PER_GROUP - 1) // 2
N_CLASSES = (N_EXPERTS // EXPERTS_PER_GROUP) * PAIRS_PER_GROUP
LN_EPS = 1e-5
GN_EPS = 1e-6
MASK_VALUE = -1e30
LOG2E = math.log2(math.e)

LANES = 128
V7X_VMEM_LIMIT_BYTES = 52 * 1024 * 1024

MIXER_ROWS = 512
FOX_BLOCK = 512
RET_ROWS = 256
HGRN_ROWS = 128
HGRN_SUB = 4
MERGE_ROWS = 512
POS_LANES = 2048
DISPATCH_ROWS = 512
EXPERT_ROWS = 512
COMBINE_ROWS = 256
DMA_ISSUE_UNROLL = 8

_W_FOX = 3 * BRANCH_W
_C_FOXF = _W_FOX
_C_RET = _C_FOXF + LANES
_C_HGF = _C_RET + 4 * BRANCH_W
_C_HGR = _C_HGF + BRANCH_W
_C_MQ = _C_HGR + 3 * BRANCH_W
_C_END = _C_MQ + BRANCH_W

_NT = (((1,), (1,)), ((), ()))


def _cparams(sem):
    return pltpu.CompilerParams(dimension_semantics=sem, vmem_limit_bytes=V7X_VMEM_LIMIT_BYTES)


def _head_id(shape):
    return lax.broadcasted_iota(I32, shape, len(shape) - 1) // HEAD_DIM


def _dot(a, b):
    return jnp.dot(a, b, preferred_element_type=F32)


def _dot_nt(a, b):
    return lax.dot_general(a, b, _NT, preferred_element_type=F32)


def _silu(x):
    return x * jax.nn.sigmoid(x)


def _layer_norm(z, g, b):
    mu = jnp.mean(z, axis=-1, keepdims=True)
    d = z - mu
    var = jnp.mean(d * d, axis=-1, keepdims=True)
    return d * lax.rsqrt(var + LN_EPS) * g + b


_FOX_BIAS_PARTS = 3


def _fox_spare_lane(h):
    return HEAD_DIM * ((h + 1) % N_HEADS)


def _fox_place_table():
    lane = jnp.arange(BRANCH_W)
    head = jnp.arange(LANES)
    dest = jnp.where(head < N_HEADS, HEAD_DIM * ((head + 1) % N_HEADS), -BRANCH_W)
    return jnp.stack([(lane[None, :] == dest[:, None] + part) for part in range(_FOX_BIAS_PARTS)]).astype(BF16)


def _key_bias_block(ff, tri, place_ref, carry_ref):
    ls = jax.nn.log_sigmoid(ff) * LOG2E
    cs = jnp.dot(tri, ls, precision=HIGHEST, preferred_element_type=F32) + carry_ref[0:1, :]
    carry_ref[...] = jnp.broadcast_to(cs[ff.shape[0] - 1:, :], carry_ref.shape)
    rest = -cs
    kb = jnp.zeros((ff.shape[0], BRANCH_W), F32)
    for part in range(_FOX_BIAS_PARTS):
        piece = rest.astype(BF16)
        rest = rest - piece.astype(F32)
        kb = kb + _dot(piece, place_ref[part])
    return kb.astype(BF16)


def _fox_lane_table():
    lane = jnp.arange(BRANCH_W)
    rows = []
    for h in range(N_HEADS):
        spare = _fox_spare_lane(h)
        bias = (lane >= spare) & (lane < spare + _FOX_BIAS_PARTS)
        rows += [lane // HEAD_DIM == h, bias, ~bias, lane == spare]
    return jnp.stack(rows).astype(F32)


def _fox_kernel(q_ref, k_ref, v_ref, kb_ref, lanes_ref, o_ref, qh_ref, m_ref, acc_ref):
    qi = pl.program_id(1)
    ki = pl.program_id(2)
    blk = q_ref.shape[1]

    def pattern(h, r):
        return lanes_ref[4 * h + r:4 * h + r + 1, :].astype(BF16)

    @pl.when(ki == 0)
    def _():
        m_ref[...] = jnp.full(m_ref.shape, MASK_VALUE, F32)
        acc_ref[...] = jnp.zeros(acc_ref.shape, F32)
        q = q_ref[0]
        for h in range(N_HEADS):
            qh_ref[h] = q * pattern(h, 0) + pattern(h, 1)

    def sweep(masked):
        k = k_ref[0]
        v = v_ref[0]
        kb = kb_ref[0]
        if masked:
            causal = lax.broadcasted_iota(I32, (blk, blk), 0) >= lax.broadcasted_iota(I32, (blk, blk), 1)
        def logits(h):
            return _dot_nt(qh_ref[h], k * pattern(h, 2) + kb * pattern(h, 1))

        def softmax_step(h, s):
            if masked:
                s = jnp.where(causal, s, MASK_VALUE)
            m_prev = m_ref[h]
            m_new = jnp.maximum(m_prev, jnp.max(s, axis=-1, keepdims=True))
            m_ref[h] = m_new
            p = jnp.exp2(s - jnp.concatenate([m_new] * (blk // LANES), axis=1))
            return p.astype(BF16), jnp.exp2(m_prev - m_new)

        def accumulate(h, p, alpha):
            ones_lane = pattern(h, 3)
            vh = v * (1 - ones_lane) + ones_lane
            acc_ref[h] = acc_ref[h] * jnp.concatenate([alpha] * (BRANCH_W // LANES), axis=1) + _dot(p, vh)

        s_next = logits(0)
        for h in range(N_HEADS):
            s = s_next
            if h + 1 < N_HEADS:
                s_next = logits(h + 1)
            p, alpha = softmax_step(h, s)
            accumulate(h, p, alpha)

    @pl.when(ki < qi)
    def _():
        sweep(False)

    @pl.when(ki == qi)
    def _():
        sweep(True)
        lane = lax.broadcasted_iota(I32, (blk, BRANCH_W), 1)
        out = jnp.zeros((blk, BRANCH_W), F32)
        for h in range(N_HEADS):
            acc = acc_ref[h]
            spare = _fox_spare_lane(h)
            out = jnp.where((lane // HEAD_DIM) == h, acc / acc[:, spare:spare + 1], out)
        o_ref[0] = out.astype(BF16)


def _fox_attention(qkv3, kb, blk):
    B, S, _ = qkv3.shape
    n = S // blk
    kv_blk = lambda c: pl.BlockSpec((1, blk, BRANCH_W), lambda b, qi, ki: (b, jnp.minimum(ki, qi), c))
    lanes = _fox_lane_table()
    return pl.pallas_call(
        _fox_kernel,
        out_shape=jax.ShapeDtypeStruct((B, S, BRANCH_W), BF16),
        grid=(B, n, n),
        in_specs=[pl.BlockSpec((1, blk, BRANCH_W), lambda b, qi, ki: (b, qi, 0)), kv_blk(1), kv_blk(2),
                  pl.BlockSpec((1, blk, BRANCH_W), lambda b, qi, ki: (b, jnp.minimum(ki, qi), 0)),
                  pl.BlockSpec(lanes.shape, lambda b, qi, ki: (0, 0))],
        out_specs=pl.BlockSpec((1, blk, BRANCH_W), lambda b, qi, ki: (b, qi, 0)),
        scratch_shapes=[pltpu.VMEM((N_HEADS, blk, BRANCH_W), BF16),
                        pltpu.VMEM((N_HEADS, blk, LANES), F32),
                        pltpu.VMEM((N_HEADS, blk, BRANCH_W), F32)],
        compiler_params=_cparams(("parallel", "parallel", "arbitrary")),
        name="fox_attention",
    )(qkv3, qkv3, qkv3, kb, lanes)


def _mem_block(q, mk, mv):
    rows = q.shape[0]
    q = q * (HEAD_DIM ** -0.5)
    hid = _head_id((rows, BRANCH_W))
    out = jnp.zeros((rows, BRANCH_W), F32)
    for h in range(N_HEADS):
        qh = jnp.where(hid == h, q, jnp.zeros_like(q))
        s = _dot_nt(qh, mk)
        p = jnp.exp(s - jnp.max(s, axis=-1, keepdims=True))
        pv = _dot(p.astype(BF16), mv)
        out = jnp.where(hid == h, pv / jnp.sum(p, axis=-1, keepdims=True), out)
    return out.astype(BF16)


def _block_diag(val_per_head):
    hid = jnp.arange(BRANCH_W) // HEAD_DIM
    same = hid[:, None] == hid[None, :]
    return jnp.where(same, jnp.asarray(val_per_head, F32)[hid][:, None], 0.0)


def _rope_tables(S):
    half = HEAD_DIM // 2
    inv = ROPE_BASE ** (-2.0 * jnp.arange(half, dtype=F32) / HEAD_DIM)
    ang = jnp.arange(S, dtype=F32)[:, None] * inv[None, :]
    cos = jnp.tile(jnp.cos(ang), (1, 2 * N_HEADS))
    sin = jnp.tile(jnp.sin(ang), (1, 2 * N_HEADS))
    d = jnp.arange(BRANCH_W)
    lo = (d % HEAD_DIM) < half
    perm = jnp.where(lo[None, :] & (d[:, None] == d[None, :] + half), -1.0, 0.0)
    perm = perm + jnp.where((~lo)[None, :] & (d[:, None] == d[None, :] - half), 1.0, 0.0)
    return cos, sin, perm.astype(BF16)


def _ret_tables(R):
    log_gamma = jnp.log1p(-jnp.exp2(-5.0 - jnp.arange(N_HEADS, dtype=F32)))
    idx = jnp.arange(R, dtype=F32)
    chunk = jnp.arange(R) // RET_CHUNK
    dist = jnp.abs(idx[:, None] - idx[None, :])
    visible = chunk[None, :] <= chunk[:, None]
    decay = jnp.where(visible[None], jnp.exp(log_gamma[:, None, None] * dist[None]), 0.0)
    lanes_lg = jnp.repeat(log_gamma, HEAD_DIM)[None, :]
    qdec = jnp.exp(lanes_lg * (idx[:, None] + 1.0))
    kdec = jnp.exp(lanes_lg * (R - 1.0 - idx[:, None]))
    sdec = _block_diag(jnp.exp(log_gamma * R))
    return decay, qdec, kdec, sdec


def _ret_block(qb, kb, v, g, cos, sin, perm, decay_ref, qdec, kdec, sdec, bd, gmean, st_ref):
    R = qb.shape[0]
    qr = qb.astype(F32) * cos + _dot(qb, perm) * sin
    kr = (kb.astype(F32) * cos + _dot(kb, perm) * sin) * (HEAD_DIM ** -0.5)
    qrb = qr.astype(BF16)
    krb = kr.astype(BF16)
    hid = _head_id((R, BRANCH_W))

    st = st_ref[...]
    o = _dot((qr * qdec).astype(BF16), st.astype(BF16))
    for h in range(N_HEADS):
        qh = jnp.where(hid == h, qrb, jnp.zeros_like(qrb))
        a = _dot_nt(qh, krb) * decay_ref[h]
        o = o + jnp.where(hid == h, _dot(a.astype(BF16), v), 0.0)
    kd_t = (kr * kdec).T.astype(BF16)
    st_ref[...] = st * sdec + _dot(kd_t, v) * bd

    mu = _dot(o.astype(BF16), gmean)
    d = o - mu
    var = _dot((d * d).astype(BF16), gmean)
    on = d * lax.rsqrt(var + GN_EPS)
    return (on * _silu(g.astype(F32))).astype(BF16)


def _hgrn_levels(R):
    levels = []
    m = HGRN_SUB
    while 2 * m <= R:
        levels.append(m)
        m *= 2
    return tuple(levels)


def _hgrn_block(zf, vb, qb, g, lb, ng, tri, seg_ref, bd, gsum, gmean, st_ref):
    R = zf.shape[0]
    W = BRANCH_W
    kf = (1.0 - lb) * jax.nn.sigmoid(-zf)
    logf = jnp.log1p(-kf)
    b = jnp.dot(tri, logf, precision=HIGHEST, preferred_element_type=F32)
    q = qb.astype(F32)
    v = vb.astype(F32)
    hid = _head_id((R, W))
    row = lax.broadcasted_iota(I32, (R, W), 0)

    o = _dot((q * kf).astype(BF16), gsum) * v
    sub = row % HGRN_SUB
    for d in range(1, HGRN_SUB):
        e = jnp.exp(b - pltpu.roll(b, d, 0))
        w = jnp.where(sub >= d, q * pltpu.roll(kf, d, 0) * e, 0.0)
        o = o + _dot(w.astype(BF16), gsum) * pltpu.roll(v, d, 0)

    a_heads = [jnp.zeros((R, R), F32) for _ in range(N_HEADS)]
    for li, m in enumerate(_hgrn_levels(R)):
        nseg = R // (2 * m)
        bnd = jnp.broadcast_to(b.reshape(nseg, 2 * m, W)[:, m - 1:m, :], (nseg, 2 * m, W)).reshape(R, W)
        second = (row % (2 * m)) >= m
        qm = jnp.where(second, q * jnp.exp(b - bnd), 0.0).astype(BF16)
        km = jnp.where(second, 0.0, kf * jnp.exp(bnd - b)).astype(BF16)
        for h in range(N_HEADS):
            qh = jnp.where(hid == h, qm, jnp.zeros_like(qm))
            pairs = _dot_nt(qh, km)
            a_heads[h] = a_heads[h] + (pairs if nseg == 1 else pairs * seg_ref[li])
    for h in range(N_HEADS):
        o = o + jnp.where(hid == h, _dot(a_heads[h].astype(BF16), vb), 0.0)

    st = st_ref[...]
    o = o + _dot_nt((q * jnp.exp(b)).astype(BF16), st.astype(BF16))
    b_last = b[R - 1:R, :]
    kd = (kf * jnp.exp(b_last - b)).astype(BF16)
    st_ref[...] = st * jnp.exp(b_last) + _dot(v.T.astype(BF16), kd) * bd

    ms = _dot((o * o).astype(BF16), gmean)
    on = o * lax.rsqrt(ms + GN_EPS) * ng
    return (on * _silu(g.astype(F32))).astype(BF16)


def _mixer_kernel(x_ref, wt_ref, b_ref, mem_ref, wkv_ref, cos_ref, sin_ref, perm_ref, decay_ref, qdec_ref,
                  kdec_ref, sdec_ref, bd_ref, gsum_ref, gmean_ref, lb_ref, ng_ref, tri_ref, seg_ref, place_ref,
                  qkv_ref, kb_ref, oret_ref, ohg_ref, omem_ref,
                  kv_ref, ret_st_ref, hg_st_ref, carry_ref):
    @pl.when(pl.program_id(1) == 0)
    def _():
        kv_ref[...] = _dot(mem_ref[0].astype(BF16), wkv_ref[...]).astype(BF16)
        ret_st_ref[...] = jnp.zeros(ret_st_ref.shape, F32)
        hg_st_ref[...] = jnp.zeros(hg_st_ref.shape, F32)
        carry_ref[...] = jnp.zeros(carry_ref.shape, F32)

    xb = x_ref[0].astype(BF16)
    tm = xb.shape[0]
    W = BRANCH_W

    def seg(lo, hi):
        return _dot_nt(xb, wt_ref[lo:hi, :]) + b_ref[:, lo:hi]

    hgf = seg(_C_HGF, _C_HGR)
    hgr = seg(_C_HGR, _C_MQ).astype(BF16)
    bd = bd_ref[...]
    gmean = gmean_ref[...]
    tri = tri_ref[...]
    R = tri.shape[0]
    for r in range(tm // R):
        rows = slice(r * R, (r + 1) * R)
        ohg_ref[0, rows, :] = _hgrn_block(hgf[rows], hgr[rows, :W], hgr[rows, W:2 * W], hgr[rows, 2 * W:],
                                          lb_ref[...], ng_ref[...], tri, seg_ref, bd, gsum_ref[...], gmean, hg_st_ref)

    ff = seg(_C_FOXF, _C_RET)
    for r in range(tm // R):
        rows = slice(r * R, (r + 1) * R)
        kb_ref[0, rows, :] = _key_bias_block(ff[rows], tri, place_ref, carry_ref)
    qkv_ref[0] = seg(0, _C_FOXF).astype(BF16)

    ret = seg(_C_RET, _C_HGF).astype(BF16)
    RR = qdec_ref.shape[0]
    for r in range(tm // RR):
        rows = slice(r * RR, (r + 1) * RR)
        oret_ref[0, rows, :] = _ret_block(ret[rows, :W], ret[rows, W:2 * W], ret[rows, 2 * W:3 * W], ret[rows, 3 * W:],
                                          cos_ref[rows, :], sin_ref[rows, :], perm_ref[...], decay_ref, qdec_ref[...],
                                          kdec_ref[...], sdec_ref[...], bd, gmean, ret_st_ref)

    mq = seg(_C_MQ, _C_END).astype(BF16)
    omem_ref[0] = _mem_block(mq, kv_ref[:, :W], kv_ref[:, W:])


def _mixer(x3, wt_mix, b_mix, mem, w_kv, lb_row, ng_row, tm, ret_rows, hgrn_rows):
    B, S, D = x3.shape
    M = mem.shape[1]
    cos, sin, perm = _rope_tables(S)
    decay, qdec, kdec, sdec = _ret_tables(ret_rows)
    bd = _block_diag(jnp.ones((N_HEADS,), F32))
    gsum = bd.astype(BF16)
    gmean = (bd / HEAD_DIM).astype(BF16)
    idx = jnp.arange(hgrn_rows)
    tri = (idx[:, None] >= idx[None, :]).astype(F32)
    seg = jnp.stack([(idx[:, None] // (2 * m) == idx[None, :] // (2 * m)).astype(F32)
                     for m in _hgrn_levels(hgrn_rows)])
    place = _fox_place_table()
    const = lambda a: pl.BlockSpec(a.shape, lambda b, j: (0,) * a.ndim)
    tile = lambda w: pl.BlockSpec((1, tm, w), lambda b, j: (b, j, 0))
    pos_tab = pl.BlockSpec((tm, BRANCH_W), lambda b, j: (j, 0))
    return pl.pallas_call(
        _mixer_kernel,
        out_shape=[jax.ShapeDtypeStruct((B, S, _C_FOXF), BF16)] + [jax.ShapeDtypeStruct((B, S, BRANCH_W), BF16)] * 4,
        grid=(B, S // tm),
        in_specs=[tile(D), const(wt_mix), const(b_mix),
                  pl.BlockSpec((1, M, D), lambda b, j: (b, 0, 0)), const(w_kv),
                  pos_tab, pos_tab, const(perm), const(decay), const(qdec), const(kdec), const(sdec),
                  const(bd), const(gsum), const(gmean), const(lb_row), const(ng_row), const(tri), const(seg),
                  const(place)],
        out_specs=[tile(_C_FOXF)] + [tile(BRANCH_W)] * 4,
        scratch_shapes=[pltpu.VMEM((M, 2 * BRANCH_W), BF16),
                        pltpu.VMEM((BRANCH_W, BRANCH_W), F32),
                        pltpu.VMEM((BRANCH_W, BRANCH_W), F32),
                        pltpu.VMEM((8, LANES), F32)],
        compiler_params=_cparams(("parallel", "arbitrary")),
        name="mixer",
    )(x3, wt_mix, b_mix, mem, w_kv, cos, sin, perm, decay, qdec, kdec, sdec, bd, gsum, gmean, lb_row, ng_row,
      tri, seg, place)


_R_W1, _R_W2, _R_E1, _R_E2, _R_CLASS, _R_RANK = range(6)


def _route_block(x1, wr_hi_ref, wr_lo_ref, br_ref, lstrict_ref, run_ref):
    tm = x1.shape[0]
    x_hi = x1.astype(BF16)
    x_lo = (x1 - x_hi.astype(F32)).astype(BF16)
    logits = (_dot(x_hi, wr_hi_ref[...]) + _dot(x_hi, wr_lo_ref[...]) + _dot(x_lo, wr_hi_ref[...])) + br_ref[...]
    lane = lax.broadcasted_iota(I32, (tm, LANES), 1).astype(F32)
    group = jnp.floor(lane * (1.0 / EXPERTS_PER_GROUP))
    e = jnp.exp(logits - jnp.max(logits, axis=-1, keepdims=True))
    p = e / jnp.sum(e, axis=-1, keepdims=True)
    p1 = jnp.max(p, axis=-1, keepdims=True)
    e1 = jnp.min(jnp.where(p == p1, lane, float(LANES)), axis=-1, keepdims=True)
    in_group = group == jnp.floor(e1 * (1.0 / EXPERTS_PER_GROUP))
    rest = jnp.where(in_group, jnp.where(lane == e1, -1.0, p), -1.0)
    p2 = jnp.max(rest, axis=-1, keepdims=True)
    e2 = jnp.min(jnp.where(rest == p2, lane, float(LANES)), axis=-1, keepdims=True)
    w1 = p1 / (p1 + p2)
    w2 = p2 / (p1 + p2)

    grp = jnp.floor(e1 * (1.0 / EXPERTS_PER_GROUP))
    lo = jnp.minimum(e1, e2) - EXPERTS_PER_GROUP * grp
    hi = jnp.maximum(e1, e2) - EXPERTS_PER_GROUP * grp
    cls = grp * PAIRS_PER_GROUP + lo * (EXPERTS_PER_GROUP - 1) - lo * (lo - 1.0) * 0.5 + (hi - lo - 1.0)
    sel = jnp.where(lane == cls, 1.0, 0.0)
    before = _dot(lstrict_ref[...], sel.astype(BF16)) + run_ref[0:1, :]
    rank = jnp.sum(jnp.where(lane == cls, before, 0.0), axis=-1, keepdims=True)
    run_ref[...] = run_ref[...] + jnp.sum(sel, axis=0, keepdims=True)

    info = jnp.zeros((tm, LANES), F32)
    for col, val in ((_R_W1, w1), (_R_W2, w2), (_R_E1, e1), (_R_E2, e2), (_R_CLASS, cls), (_R_RANK, rank)):
        info = jnp.where(lane == float(col), val, info)
    pick = (lax.broadcasted_iota(I32, (8, LANES), 0) == lax.broadcasted_iota(I32, (8, LANES), 1)).astype(F32)
    rows = lax.dot_general(pick, info, _NT, precision=HIGHEST, preferred_element_type=F32)
    return info, rows


def _merge_kernel(x_ref, b0_ref, b1_ref, b2_ref, b3_ref, wgt_ref, bg_ref, wb_ref, wo_ref, lng_ref, lnb_ref,
                  wr_hi_ref, wr_lo_ref, br_ref, lstrict_ref, o_ref, info_ref, rows_ref, cnt_ref, run_ref, *, alpha):
    @pl.when(pl.program_id(0) == 0)
    def _():
        run_ref[...] = jnp.zeros(run_ref.shape, F32)

    x = x_ref[...]
    xb = x.astype(BF16)
    D = x.shape[1]
    mixed = jnp.zeros(x.shape, F32)
    for n, br in enumerate((b0_ref, b1_ref, b2_ref, b3_ref)):
        gate = jax.nn.sigmoid(_dot_nt(xb, wgt_ref[n * D:(n + 1) * D, :]) + bg_ref[:, n * D:(n + 1) * D])
        mixed = mixed + gate * _dot(br[...], wb_ref[n])
    z = alpha * x + _dot(mixed.astype(BF16), wo_ref[...])
    x1 = _layer_norm(z, lng_ref[...], lnb_ref[...])
    o_ref[...] = x1
    info, rows = _route_block(x1, wr_hi_ref, wr_lo_ref, br_ref, lstrict_ref, run_ref)
    info_ref[...] = info
    rows_ref[...] = rows
    cnt_ref[...] = run_ref[...]


def _merge(x2, branches, wgt, bg, wb, wo, lng, lnb, wr_pad, br_pad, alpha, tm):
    T, D = x2.shape
    const = lambda a: pl.BlockSpec(a.shape, lambda i: (0,) * a.ndim)
    br_spec = pl.BlockSpec((tm, BRANCH_W), lambda i: (i, 0))
    idx = jnp.arange(tm)
    lstrict = (idx[:, None] > idx[None, :]).astype(BF16)
    wr_hi = wr_pad.astype(BF16)
    wr_lo = (wr_pad - wr_hi.astype(F32)).astype(BF16)
    return pl.pallas_call(
        functools.partial(_merge_kernel, alpha=alpha),
        out_shape=[jax.ShapeDtypeStruct((T, D), F32), jax.ShapeDtypeStruct((T, LANES), F32),
                   jax.ShapeDtypeStruct((8, T), F32), jax.ShapeDtypeStruct((8, LANES), F32)],
        grid=(T // tm,),
        in_specs=[pl.BlockSpec((tm, D), lambda i: (i, 0)), br_spec, br_spec, br_spec, br_spec,
                  const(wgt), const(bg), const(wb), const(wo), const(lng), const(lnb),
                  const(wr_hi), const(wr_lo), const(br_pad), const(lstrict)],
        out_specs=[pl.BlockSpec((tm, D), lambda i: (i, 0)),
                   pl.BlockSpec((tm, LANES), lambda i: (i, 0)),
                   pl.BlockSpec((8, tm), lambda i: (0, i)),
                   pl.BlockSpec((8, LANES), lambda i: (0, 0))],
        scratch_shapes=[pltpu.VMEM((8, LANES), F32)],
        compiler_params=_cparams(("arbitrary",)),
        name="merge_ln_route",
    )(x2, *branches, wgt, bg, wb, wo, lng, lnb, wr_hi, wr_lo, br_pad, lstrict)


def _pos_kernel(off_ref, rows_ref, pos_ref):
    rows = rows_ref[...]
    cls = rows[_R_CLASS:_R_CLASS + 1, :]
    start = jnp.zeros(cls.shape, F32)
    for c in range(N_CLASSES):
        start = jnp.where(cls == float(c), off_ref[c].astype(F32), start)
    pos_ref[...] = jnp.zeros(pos_ref.shape, I32)
    pos_ref[0:1, :] = (rows[_R_RANK:_R_RANK + 1, :] + start).astype(I32)


def _positions(offsets, rows, tl):
    T = rows.shape[1]
    return pl.pallas_call(
        _pos_kernel,
        out_shape=jax.ShapeDtypeStruct((8, T), I32),
        grid_spec=pltpu.PrefetchScalarGridSpec(
            num_scalar_prefetch=1, grid=(T // tl,),
            in_specs=[pl.BlockSpec((8, tl), lambda i, off: (0, i))],
            out_specs=pl.BlockSpec((8, tl), lambda i, off: (0, i))),
        compiler_params=_cparams(("parallel",)),
        name="moe_positions",
    )(offsets, rows)


def _from_slab(ref, idx, rows, slab):
    return jnp.concatenate([ref[idx + (pl.ds(s, rows, stride=slab), slice(None))] for s in range(slab)], axis=1)


def _dispatch_kernel(pad_lo_ref, pad_hi_ref, nu_ref, pos_ref, x_ref, xs_ref, slab_ref, zero_ref, sem, *, bm):
    td, D = x_ref.shape
    slab = D // LANES
    i = pl.program_id(0)
    slot = i % 2

    def wait_slot(s):
        pltpu.make_async_copy(slab_ref.at[s], xs_ref.at[pl.ds(0, td * slab)], sem.at[s]).wait()

    @pl.when(i >= 2)
    def _():
        wait_slot(slot)

    x = x_ref[...]
    for s in range(slab):
        slab_ref[slot, pl.ds(s, td, stride=slab), :] = x[:, s * LANES:(s + 1) * LANES]

    def issue(t, carry):
        src = slab_ref.at[slot, pl.ds(pl.multiple_of(t * slab, slab), slab)]
        dst = xs_ref.at[pl.ds(pl.multiple_of(pos_ref[t] * slab, slab), slab)]
        pltpu.make_async_copy(src, dst, sem.at[slot]).start()
        return carry

    lax.fori_loop(0, td, issue, 0, unroll=DMA_ISSUE_UNROLL)

    last = pl.num_programs(0) - 1

    @pl.when(i == last)
    def _():
        wait_slot(slot)

    @pl.when(jnp.logical_and(i == last, i >= 1))
    def _():
        wait_slot(1 - slot)

    @pl.when(i == last)
    def _():
        zero_ref[...] = jnp.zeros(zero_ref.shape, F32)
        zero_row = zero_ref.at[pl.ds(0, slab)]
        n_blocks = xs_ref.shape[0] // (bm * slab)

        def row_copy(r):
            return pltpu.make_async_copy(zero_row, xs_ref.at[pl.ds(pl.multiple_of(r * slab, slab), slab)], sem.at[2])

        def block_copy(blk):
            start = pl.multiple_of(blk * (bm * slab), bm * slab)
            return pltpu.make_async_copy(zero_ref, xs_ref.at[pl.ds(start, bm * slab)], sem.at[2])

        def start_row(r, carry):
            row_copy(r).start()
            return carry

        def wait_row(r, carry):
            row_copy(r).wait()
            return carry

        def start_block(blk, carry):
            block_copy(blk).start()
            return carry

        def wait_block(blk, carry):
            block_copy(blk).wait()
            return carry

        for c in range(N_CLASSES):
            lax.fori_loop(pad_lo_ref[c], pad_hi_ref[c], start_row, 0)
        lax.fori_loop(nu_ref[0], n_blocks, start_block, 0)
        for c in range(N_CLASSES):
            lax.fori_loop(pad_lo_ref[c], pad_hi_ref[c], wait_row, 0)
        lax.fori_loop(nu_ref[0], n_blocks, wait_block, 0)


def _dispatch(pad_lo, pad_hi, n_used, pos, x2, n_slots, td, bm):
    T, D = x2.shape
    slab = D // LANES
    smem_blk = pl.BlockSpec((td,), lambda i, lo, hi, nu: (i,), memory_space=pltpu.SMEM)
    return pl.pallas_call(
        functools.partial(_dispatch_kernel, bm=bm),
        out_shape=jax.ShapeDtypeStruct((n_slots * slab, LANES), F32),
        grid_spec=pltpu.PrefetchScalarGridSpec(
            num_scalar_prefetch=3, grid=(T // td,),
            in_specs=[smem_blk, pl.BlockSpec((td, D), lambda i, lo, hi, nu: (i, 0))],
            out_specs=pl.BlockSpec(memory_space=pl.ANY),
            scratch_shapes=[pltpu.VMEM((2, td * slab, LANES), F32), pltpu.VMEM((bm * slab, LANES), F32),
                            pltpu.SemaphoreType.DMA((3,))]),
        compiler_params=_cparams(("arbitrary",)),
        name="moe_dispatch",
    )(pad_lo, pad_hi, n_used, pos, x2)


def _expert_kernel(ea_ref, eb_ref, nu_ref, xs_ref, wia_ref, woa_ref, wib_ref, wob_ref, ys_ref):
    del ea_ref, eb_ref
    used = pl.program_id(0) < nu_ref[0]

    @pl.when(used)
    def _():
        F = woa_ref.shape[1]
        slab = woa_ref.shape[2] // LANES
        bm = xs_ref.shape[0] // slab
        xb = _from_slab(xs_ref, (), bm, slab).astype(BF16)
        for half, (wi_ref, wo_ref) in enumerate(((wia_ref, woa_ref), (wib_ref, wob_ref))):
            h = _dot(xb, wi_ref[0])
            act = _silu(h[:, :F]) * h[:, F:]
            y = _dot(act.astype(BF16), wo_ref[0])
            for s in range(slab):
                ys_ref[pl.ds(half * slab + s, bm, stride=2 * slab), :] = y[:, s * LANES:(s + 1) * LANES]

    @pl.when(jnp.logical_not(used))
    def _():
        ys_ref[...] = jnp.zeros(ys_ref.shape, F32)


def _experts(block_a, block_b, n_used, xs, w_in, w_out, bm):
    E, D, F2 = w_in.shape
    slab = D // LANES
    n_blocks = xs.shape[0] // (bm * slab)
    row_blk = lambda i, ea, eb, nu: (jnp.maximum(jnp.minimum(i, nu[0] - 1), 0), 0)
    w_in_of = lambda table: pl.BlockSpec((1, D, F2), lambda i, ea, eb, nu: ((ea, eb)[table][i], 0, 0))
    w_out_of = lambda table: pl.BlockSpec((1, F2 // 2, D), lambda i, ea, eb, nu: ((ea, eb)[table][i], 0, 0))
    return pl.pallas_call(
        _expert_kernel,
        out_shape=jax.ShapeDtypeStruct((2 * xs.shape[0], LANES), F32),
        grid_spec=pltpu.PrefetchScalarGridSpec(
            num_scalar_prefetch=3, grid=(n_blocks,),
            in_specs=[pl.BlockSpec((bm * slab, LANES), row_blk), w_in_of(0), w_out_of(0), w_in_of(1), w_out_of(1)],
            out_specs=pl.BlockSpec((2 * bm * slab, LANES), lambda i, ea, eb, nu: (i, 0))),
        compiler_params=_cparams(("arbitrary",)),
        name="moe_experts",
    )(block_a, block_b, n_used, xs, w_in, w_out, w_in, w_out)


def _combine_kernel(pos_ref, nxt_ref, info_ref, x_ref, ys_ref, lng_ref, lnb_ref, o_ref, buf_ref, sem, *, alpha):
    tc, D = x_ref.shape
    slab2 = 2 * (D // LANES)
    i = pl.program_id(0)
    slot = i % 2

    def gather(p_ref, into):
        def issue(t, carry):
            src = ys_ref.at[pl.ds(pl.multiple_of(p_ref[t] * slab2, slab2), slab2)]
            dst = buf_ref.at[into, pl.ds(pl.multiple_of(t * slab2, slab2), slab2)]
            pltpu.make_async_copy(src, dst, sem.at[into]).start()
            return carry

        lax.fori_loop(0, tc, issue, 0, unroll=DMA_ISSUE_UNROLL)

    @pl.when(i == 0)
    def _():
        gather(pos_ref, slot)

    @pl.when(i + 1 < pl.num_programs(0))
    def _():
        gather(nxt_ref, 1 - slot)

    pltpu.make_async_copy(ys_ref.at[pl.ds(0, tc * slab2)], buf_ref.at[slot], sem.at[slot]).wait()
    info = info_ref[...]
    both = _from_slab(buf_ref, (slot,), tc, slab2)
    w1 = info[:, _R_W1:_R_W1 + 1]
    w2 = info[:, _R_W2:_R_W2 + 1]
    first_is_lower = info[:, _R_E1:_R_E1 + 1] < info[:, _R_E2:_R_E2 + 1]
    moe = jnp.where(first_is_lower, w1, w2) * both[:, :D] + jnp.where(first_is_lower, w2, w1) * both[:, D:]
    o_ref[...] = _layer_norm(alpha * x_ref[...] + moe, lng_ref[...], lnb_ref[...])


def _combine(pos, info, x2, ys, lng, lnb, alpha, tc):
    T, D = x2.shape
    n = T // tc
    cur = pl.BlockSpec((tc,), lambda i: (i,), memory_space=pltpu.SMEM)
    nxt = pl.BlockSpec((tc,), lambda i: (jnp.minimum(i + 1, n - 1),), memory_space=pltpu.SMEM)
    return pl.pallas_call(
        functools.partial(_combine_kernel, alpha=alpha),
        out_shape=jax.ShapeDtypeStruct((T, D), F32),
        grid=(n,),
        in_specs=[cur, nxt,
                  pl.BlockSpec((tc, LANES), lambda i: (i, 0)),
                  pl.BlockSpec((tc, D), lambda i: (i, 0)),
                  pl.BlockSpec(memory_space=pl.ANY),
                  pl.BlockSpec((1, D), lambda i: (0, 0)),
                  pl.BlockSpec((1, D), lambda i: (0, 0))],
        out_specs=pl.BlockSpec((tc, D), lambda i: (i, 0)),
        scratch_shapes=[pltpu.VMEM((2, 2 * tc * (D // LANES), LANES), F32), pltpu.SemaphoreType.DMA((2,))],
        compiler_params=_cparams(("arbitrary",)),
        name="moe_combine",
    )(pos, pos, info, x2, ys, lng, lnb)


def _moe(x2, info, rows, counts, w_e_in, w_e_out, lng, lnb, alpha, tiles):
    T, D = x2.shape
    bm = tiles["expert"]
    cnt = counts[0, :N_CLASSES].astype(I32)
    padded = ((cnt + bm - 1) // bm) * bm
    ends = jnp.cumsum(padded)
    offsets = ends - padded
    n_blocks = T // bm + N_CLASSES
    n_used = (ends[-1] // bm).astype(I32)
    blk_start = jnp.arange(n_blocks, dtype=I32) * bm
    block_class = jnp.sum((blk_start[:, None] >= ends[None, :]).astype(I32), axis=1)
    last_class = jnp.sum((blk_start[n_used - 1] >= ends).astype(I32))
    block_class = jnp.where(jnp.arange(n_blocks) < n_used, block_class, last_class)
    pairs = [(lo, hi) for lo in range(EXPERTS_PER_GROUP) for hi in range(lo + 1, EXPERTS_PER_GROUP)]
    groups = range(N_EXPERTS // EXPERTS_PER_GROUP)
    lower = jnp.asarray([g * EXPERTS_PER_GROUP + lo for g in groups for lo, _ in pairs], I32)
    higher = jnp.asarray([g * EXPERTS_PER_GROUP + hi for g in groups for _, hi in pairs], I32)

    pos = _positions(offsets.astype(I32), rows, tiles["pos"])[0]
    n_used = n_used.reshape(1)
    xs = _dispatch((offsets + cnt).astype(I32), ends.astype(I32), n_used, pos, x2, n_blocks * bm,
                   tiles["dispatch"], bm)
    ys = _experts(lower[block_class], higher[block_class], n_used, xs, w_e_in, w_e_out, bm)
    return _combine(pos, info, x2, ys, lng, lnb, alpha, tiles["combine"])


def _tiles(B, S):
    T = B * S
    pick = lambda want, n: math.gcd(want, n)
    return dict(mixer=pick(MIXER_ROWS, S), fox=pick(FOX_BLOCK, S), ret=pick(RET_ROWS, S), hgrn=pick(HGRN_ROWS, S),
                merge=pick(MERGE_ROWS, T),
                pos=pick(POS_LANES, T), dispatch=pick(DISPATCH_ROWS, T), expert=pick(EXPERT_ROWS, T),
                combine=pick(COMBINE_ROWS, T))


def _pack_input_weights(w_in, b_in):
    c = 3 * BRANCH_W
    n_mix = c + N_HEADS + 9 * BRANCH_W
    col_scale = jnp.ones((w_in.shape[-1],), F32).at[:BRANCH_W].set(HEAD_DIM ** -0.5 * LOG2E)
    wt = jnp.transpose(w_in * col_scale, (2, 0, 1))
    b = b_in * col_scale
    packed = []
    for l in range(w_in.shape[0]):
        w = wt[:, l, :]
        wt_mix = jnp.concatenate([w[:c + N_HEADS].astype(BF16), jnp.zeros((LANES - N_HEADS, w.shape[1]), BF16),
                                  w[c + N_HEADS:n_mix].astype(BF16)], axis=0)
        b_mix = jnp.concatenate([b[l, :c + N_HEADS], jnp.zeros((LANES - N_HEADS,), F32), b[l, c + N_HEADS:n_mix]])
        packed.append((wt_mix, b_mix[None, :], w[n_mix:].astype(BF16), b[l][None, n_mix:]))
    return packed


def kernel(x, mem, w_in, b_in, w_mem_kv, hgrn_lb, hgrn_norm_g, w_branch, w_out, ln_g, ln_b, w_router, b_router,
           w_e_in, w_e_out):
    B, S, D = x.shape
    T = B * S
    depth = w_in.shape[0]
    alpha = (2.0 * depth) ** 0.25
    tiles = _tiles(B, S)

    lb_all = jax.nn.softmax(hgrn_lb.astype(F32), axis=0)
    lb_all = jnp.cumsum(lb_all, axis=0) - lb_all[0:1]
    wr_pad = jnp.concatenate([w_router, jnp.zeros((D, LANES - N_EXPERTS), F32)], axis=1)
    br_pad = jnp.concatenate([b_router, jnp.full((LANES - N_EXPERTS,), MASK_VALUE, F32)])[None, :]

    x2 = x.reshape(T, D)
    packed = _pack_input_weights(w_in, b_in)
    for l in range(depth):
        wt_mix, b_mix, wt_gate, b_gate = packed[l]
        qkv, kb, o_ret, o_hg, o_mem = _mixer(x2.reshape(B, S, D), wt_mix, b_mix, mem, w_mem_kv[l].astype(BF16),
                                             lb_all[l][None, :], jnp.tile(hgrn_norm_g[l], N_HEADS)[None, :],
                                             tiles["mixer"], tiles["ret"], tiles["hgrn"])
        o_fox = _fox_attention(qkv, kb, tiles["fox"])
        branches = [o.reshape(T, BRANCH_W) for o in (o_fox, o_ret, o_hg, o_mem)]
        x2, info, rows, counts = _merge(x2, branches, wt_gate, b_gate, w_branch[l].astype(BF16),
                                        w_out[l].astype(BF16), ln_g[l, 0][None, :], ln_b[l, 0][None, :],
                                        wr_pad, br_pad, alpha, tiles["merge"])
        x2 = _moe(x2, info, rows, counts, w_e_in[l].astype(BF16), w_e_out[l].astype(BF16),
                  ln_g[l, 1][None, :], ln_b[l, 1][None, :], alpha, tiles)
    return x2.reshape(B, S, D)
```

```python
import functools
import math

import jax
import jax.numpy as jnp
from jax import lax
from jax.experimental import pallas as pl
from jax.experimental.pallas import tpu as pltpu

F32 = jnp.float32
BF16 = jnp.bfloat16
I32 = jnp.int32
HIGHEST = lax.Precision.HIGHEST

N_HEADS = 4
HEAD_DIM = 64
BRANCH_W = N_HEADS * HEAD_DIM
N_BRANCH = 4
RET_CHUNK = 64
ROPE_BASE = 10000.0
N_EXPERTS = 16
EXPERTS_PER_GROUP = 4
PAIRS_PER_GROUP = EXPERTS_PER_GROUP * (EXPERTS_PER_GROUP - 1) // 2
N_CLASSES = (N_EXPERTS // EXPERTS_PER_GROUP) * PAIRS_PER_GROUP
LN_EPS = 1e-5
GN_EPS = 1e-6
MASK_VALUE = -1e30
LOG2E = math.log2(math.e)

LANES = 128
V7X_VMEM_LIMIT_BYTES = 52 * 1024 * 1024

MIXER_ROWS = 512
MIXER_BATCH = 2
FOX_BLOCK = 512
RET_ROWS = 256
HGRN_ROWS = 128
HGRN_SUB = 4
MERGE_ROWS = 512
POS_LANES = 2048
DISPATCH_ROWS = 512
EXPERT_ROWS = 512
COMBINE_ROWS = 256
DMA_ISSUE_UNROLL = 8

_W_FOX = 3 * BRANCH_W
_C_FOXF = _W_FOX
_C_RET = _C_FOXF + LANES
_C_HGF = _C_RET + 4 * BRANCH_W
_C_HGR = _C_HGF + BRANCH_W
_C_MQ = _C_HGR + 3 * BRANCH_W
_C_END = _C_MQ + BRANCH_W

_NT = (((1,), (1,)), ((), ()))


def _cparams(sem):
    return pltpu.CompilerParams(dimension_semantics=sem, vmem_limit_bytes=V7X_VMEM_LIMIT_BYTES)


def _head_id(shape):
    return lax.broadcasted_iota(I32, shape, len(shape) - 1) // HEAD_DIM


def _dot(a, b):
    return jnp.dot(a, b, preferred_element_type=F32)


def _dot_nt(a, b):
    return lax.dot_general(a, b, _NT, preferred_element_type=F32)


def _silu(x):
    return x * jax.nn.sigmoid(x)


def _layer_norm(z, g, b):
    mu = jnp.mean(z, axis=-1, keepdims=True)
    d = z - mu
    var = jnp.mean(d * d, axis=-1, keepdims=True)
    return d * lax.rsqrt(var + LN_EPS) * g + b


_FOX_BIAS_PARTS = 3


def _fox_spare_lane(h):
    return HEAD_DIM * ((h + 1) % N_HEADS)


def _fox_place_table():
    lane = jnp.arange(BRANCH_W)
    head = jnp.arange(LANES)
    dest = jnp.where(head < N_HEADS, HEAD_DIM * ((head + 1) % N_HEADS), -BRANCH_W)
    return jnp.stack([(lane[None, :] == dest[:, None] + part) for part in range(_FOX_BIAS_PARTS)]).astype(BF16)


def _key_bias_block(ff, tri, place_ref, carry_ref):
    ls = jax.nn.log_sigmoid(ff) * LOG2E
    cs = jnp.dot(tri, ls, precision=HIGHEST, preferred_element_type=F32) + carry_ref[0:1, :]
    carry_ref[...] = jnp.broadcast_to(cs[ff.shape[0] - 1:, :], carry_ref.shape)
    yield
    rest = -cs
    kb = jnp.zeros((ff.shape[0], BRANCH_W), F32)
    for part in range(_FOX_BIAS_PARTS):
        piece = rest.astype(BF16)
        rest = rest - piece.astype(F32)
        kb = kb + _dot(piece, place_ref[part])
    return kb.astype(BF16)


def _fox_lane_table():
    lane = jnp.arange(BRANCH_W)
    rows = []
    for h in range(N_HEADS):
        spare = _fox_spare_lane(h)
        bias = (lane >= spare) & (lane < spare + _FOX_BIAS_PARTS)
        rows += [lane // HEAD_DIM == h, bias, ~bias, lane == spare]
    return jnp.stack(rows).astype(F32)


def _fox_kernel(q_ref, k_ref, v_ref, kb_ref, lanes_ref, o_ref, qh_ref, m_ref, acc_ref):
    qi = pl.program_id(1)
    ki = pl.program_id(2)
    blk = q_ref.shape[1]

    def pattern(h, r):
        return lanes_ref[4 * h + r:4 * h + r + 1, :].astype(BF16)

    @pl.when(ki == 0)
    def _():
        m_ref[...] = jnp.full(m_ref.shape, MASK_VALUE, F32)
        acc_ref[...] = jnp.zeros(acc_ref.shape, F32)
        q = q_ref[0]
        for h in range(N_HEADS):
            qh_ref[h] = q * pattern(h, 0) + pattern(h, 1)

    def sweep(masked):
        k = k_ref[0]
        v = v_ref[0]
        kb = kb_ref[0]
        if masked:
            causal = lax.broadcasted_iota(I32, (blk, blk), 0) >= lax.broadcasted_iota(I32, (blk, blk), 1)
        def logits(h):
            return _dot_nt(qh_ref[h], k * pattern(h, 2) + kb * pattern(h, 1))

        def softmax_step(h, s):
            if masked:
                s = jnp.where(causal, s, MASK_VALUE)
            m_prev = m_ref[h]
            m_new = jnp.maximum(m_prev, jnp.max(s, axis=-1, keepdims=True))
            m_ref[h] = m_new
            p = jnp.exp2(s - jnp.concatenate([m_new] * (blk // LANES), axis=1))
            return p.astype(BF16), jnp.exp2(m_prev - m_new)

        def accumulate(h, p, alpha):
            ones_lane = pattern(h, 3)
            vh = v * (1 - ones_lane) + ones_lane
            acc_ref[h] = acc_ref[h] * jnp.concatenate([alpha] * (BRANCH_W // LANES), axis=1) + _dot(p, vh)

        s_next = logits(0)
        for h in range(N_HEADS):
            s = s_next
            if h + 1 < N_HEADS:
                s_next = logits(h + 1)
            p, alpha = softmax_step(h, s)
            accumulate(h, p, alpha)

    @pl.when(ki < qi)
    def _():
        sweep(False)

    @pl.when(ki == qi)
    def _():
        sweep(True)
        lane = lax.broadcasted_iota(I32, (blk, BRANCH_W), 1)
        out = jnp.zeros((blk, BRANCH_W), F32)
        for h in range(N_HEADS):
            acc = acc_ref[h]
            spare = _fox_spare_lane(h)
            out = jnp.where((lane // HEAD_DIM) == h, acc / acc[:, spare:spare + 1], out)
        o_ref[0] = out.astype(BF16)


def _fox_attention(qkv3, kb, blk):
    B, S, _ = qkv3.shape
    n = S // blk
    kv_blk = lambda c: pl.BlockSpec((1, blk, BRANCH_W), lambda b, qi, ki: (b, jnp.minimum(ki, qi), c))
    lanes = _fox_lane_table()
    return pl.pallas_call(
        _fox_kernel,
        out_shape=jax.ShapeDtypeStruct((B, S, BRANCH_W), BF16),
        grid=(B, n, n),
        in_specs=[pl.BlockSpec((1, blk, BRANCH_W), lambda b, qi, ki: (b, qi, 0)), kv_blk(1), kv_blk(2),
                  pl.BlockSpec((1, blk, BRANCH_W), lambda b, qi, ki: (b, jnp.minimum(ki, qi), 0)),
                  pl.BlockSpec(lanes.shape, lambda b, qi, ki: (0, 0))],
        out_specs=pl.BlockSpec((1, blk, BRANCH_W), lambda b, qi, ki: (b, qi, 0)),
        scratch_shapes=[pltpu.VMEM((N_HEADS, blk, BRANCH_W), BF16),
                        pltpu.VMEM((N_HEADS, blk, LANES), F32),
                        pltpu.VMEM((N_HEADS, blk, BRANCH_W), F32)],
        compiler_params=_cparams(("parallel", "parallel", "arbitrary")),
        name="fox_attention",
    )(qkv3, qkv3, qkv3, kb, lanes)


def _mem_block(q, mk, mv):
    rows = q.shape[0]
    q = q * (HEAD_DIM ** -0.5)
    hid = _head_id((rows, BRANCH_W))
    out = jnp.zeros((rows, BRANCH_W), F32)
    for h in range(N_HEADS):
        qh = jnp.where(hid == h, q, jnp.zeros_like(q))
        s = _dot_nt(qh, mk)
        p = jnp.exp(s - jnp.max(s, axis=-1, keepdims=True))
        pv = _dot(p.astype(BF16), mv)
        out = jnp.where(hid == h, pv / jnp.sum(p, axis=-1, keepdims=True), out)
        yield
    return out.astype(BF16)


def _block_diag(val_per_head):
    hid = jnp.arange(BRANCH_W) // HEAD_DIM
    same = hid[:, None] == hid[None, :]
    return jnp.where(same, jnp.asarray(val_per_head, F32)[hid][:, None], 0.0)


def _rope_tables(S):
    half = HEAD_DIM // 2
    inv = ROPE_BASE ** (-2.0 * jnp.arange(half, dtype=F32) / HEAD_DIM)
    ang = jnp.arange(S, dtype=F32)[:, None] * inv[None, :]
    cos = jnp.tile(jnp.cos(ang), (1, 2 * N_HEADS))
    sin = jnp.tile(jnp.sin(ang), (1, 2 * N_HEADS))
    d = jnp.arange(BRANCH_W)
    lo = (d % HEAD_DIM) < half
    perm = jnp.where(lo[None, :] & (d[:, None] == d[None, :] + half), -1.0, 0.0)
    perm = perm + jnp.where((~lo)[None, :] & (d[:, None] == d[None, :] - half), 1.0, 0.0)
    return cos, sin, perm.astype(BF16)


def _ret_tables(R):
    log_gamma = jnp.log1p(-jnp.exp2(-5.0 - jnp.arange(N_HEADS, dtype=F32)))
    idx = jnp.arange(R, dtype=F32)
    chunk = jnp.arange(R) // RET_CHUNK
    dist = jnp.abs(idx[:, None] - idx[None, :])
    visible = chunk[None, :] <= chunk[:, None]
    decay = jnp.where(visible[None], jnp.exp(log_gamma[:, None, None] * dist[None]), 0.0)
    lanes_lg = jnp.repeat(log_gamma, HEAD_DIM)[None, :]
    qdec = jnp.exp(lanes_lg * (idx[:, None] + 1.0))
    kdec = jnp.exp(lanes_lg * (R - 1.0 - idx[:, None]))
    sdec = _block_diag(jnp.exp(log_gamma * R))
    return decay, qdec, kdec, sdec


def _ret_block(qb, kb, v, g, cos, sin, perm, decay_ref, qdec, kdec, sdec, bd, gmean, st_ref):
    R = qb.shape[0]
    qr = qb.astype(F32) * cos + _dot(qb, perm) * sin
    kr = (kb.astype(F32) * cos + _dot(kb, perm) * sin) * (HEAD_DIM ** -0.5)
    qrb = qr.astype(BF16)
    krb = kr.astype(BF16)
    hid = _head_id((R, BRANCH_W))
    yield

    st = st_ref[...]
    o = _dot((qr * qdec).astype(BF16), st.astype(BF16))
    for h in range(N_HEADS):
        qh = jnp.where(hid == h, qrb, jnp.zeros_like(qrb))
        a = _dot_nt(qh, krb) * decay_ref[h]
        o = o + jnp.where(hid == h, _dot(a.astype(BF16), v), 0.0)
        yield
    kd_t = (kr * kdec).T.astype(BF16)
    st_ref[...] = st * sdec + _dot(kd_t, v) * bd
    yield

    mu = _dot(o.astype(BF16), gmean)
    d = o - mu
    var = _dot((d * d).astype(BF16), gmean)
    on = d * lax.rsqrt(var + GN_EPS)
    return (on * _silu(g.astype(F32))).astype(BF16)


def _hgrn_levels(R):
    levels = []
    m = HGRN_SUB
    while 2 * m <= R:
        levels.append(m)
        m *= 2
    return tuple(levels)


def _hgrn_block(zf, vb, qb, g, lb, ng, tri, seg_ref, bd, gsum, gmean, st_ref):
    R = zf.shape[0]
    W = BRANCH_W
    kf = (1.0 - lb) * jax.nn.sigmoid(-zf)
    logf = jnp.log1p(-kf)
    b = jnp.dot(tri, logf, precision=HIGHEST, preferred_element_type=F32)
    yield
    q = qb.astype(F32)
    v = vb.astype(F32)
    hid = _head_id((R, W))
    row = lax.broadcasted_iota(I32, (R, W), 0)

    o = _dot((q * kf).astype(BF16), gsum) * v
    sub = row % HGRN_SUB
    for d in range(1, HGRN_SUB):
        e = jnp.exp(b - pltpu.roll(b, d, 0))
        w = jnp.where(sub >= d, q * pltpu.roll(kf, d, 0) * e, 0.0)
        o = o + _dot(w.astype(BF16), gsum) * pltpu.roll(v, d, 0)
        yield

    a_heads = [jnp.zeros((R, R), F32) for _ in range(N_HEADS)]
    for li, m in enumerate(_hgrn_levels(R)):
        nseg = R // (2 * m)
        bnd = jnp.broadcast_to(b.reshape(nseg, 2 * m, W)[:, m - 1:m, :], (nseg, 2 * m, W)).reshape(R, W)
        second = (row % (2 * m)) >= m
        qm = jnp.where(second, q * jnp.exp(b - bnd), 0.0).astype(BF16)
        km = jnp.where(second, 0.0, kf * jnp.exp(bnd - b)).astype(BF16)
        for h in range(N_HEADS):
            qh = jnp.where(hid == h, qm, jnp.zeros_like(qm))
            pairs = _dot_nt(qh, km)
            a_heads[h] = a_heads[h] + (pairs if nseg == 1 else pairs * seg_ref[li])
        yield
    for h in range(N_HEADS):
        o = o + jnp.where(hid == h, _dot(a_heads[h].astype(BF16), vb), 0.0)

    yield
    st = st_ref[...]
    o = o + _dot_nt((q * jnp.exp(b)).astype(BF16), st.astype(BF16))
    b_last = b[R - 1:R, :]
    kd = (kf * jnp.exp(b_last - b)).astype(BF16)
    st_ref[...] = st * jnp.exp(b_last) + _dot(v.T.astype(BF16), kd) * bd

    yield
    ms = _dot((o * o).astype(BF16), gmean)
    on = o * lax.rsqrt(ms + GN_EPS) * ng
    return (on * _silu(g.astype(F32))).astype(BF16)


def _run_interleaved(gens):
    results = [None] * len(gens)
    live = list(range(len(gens)))
    while live:
        for i in list(live):
            try:
                next(gens[i])
            except StopIteration as done:
                results[i] = done.value
                live.remove(i)
    return results


def _mixer_kernel(x_ref, wt_ref, b_ref, mem_ref, wkv_ref, cos_ref, sin_ref, perm_ref, decay_ref, qdec_ref,
                  kdec_ref, sdec_ref, bd_ref, gsum_ref, gmean_ref, lb_ref, ng_ref, tri_ref, seg_ref, place_ref,
                  qkv_ref, kb_ref, oret_ref, ohg_ref, omem_ref,
                  kv_ref, ret_st_ref, hg_st_ref, carry_ref):
    G, tm, D = x_ref.shape

    @pl.when(pl.program_id(1) == 0)
    def _():
        for g in range(G):
            kv_ref[g] = _dot(mem_ref[g].astype(BF16), wkv_ref[...]).astype(BF16)
        ret_st_ref[...] = jnp.zeros(ret_st_ref.shape, F32)
        hg_st_ref[...] = jnp.zeros(hg_st_ref.shape, F32)
        carry_ref[...] = jnp.zeros(carry_ref.shape, F32)

    xb = x_ref[...].reshape(G * tm, D).astype(BF16)
    W = BRANCH_W

    def seg(lo, hi):
        return _dot_nt(xb, wt_ref[lo:hi, :]) + b_ref[:, lo:hi]

    def tile_rows(g, r, n):
        return slice(g * tm + r * n, g * tm + (r + 1) * n), slice(r * n, (r + 1) * n)

    hgf = seg(_C_HGF, _C_HGR)
    hgr = seg(_C_HGR, _C_MQ).astype(BF16)
    ff = seg(_C_FOXF, _C_RET)
    ret = seg(_C_RET, _C_HGF).astype(BF16)
    mq = seg(_C_MQ, _C_END).astype(BF16)
    qkv = seg(0, _C_FOXF).astype(BF16)
    for g in range(G):
        qkv_ref[g] = qkv[g * tm:(g + 1) * tm]
    bd = bd_ref[...]
    gmean = gmean_ref[...]
    tri = tri_ref[...]
    R = tri.shape[0]
    RR = qdec_ref.shape[0]

    def hgrn_chain(g):
        for r in range(tm // R):
            src, dst = tile_rows(g, r, R)
            ohg_ref[g, dst, :] = yield from _hgrn_block(hgf[src], hgr[src, :W], hgr[src, W:2 * W], hgr[src, 2 * W:],
                                                        lb_ref[...], ng_ref[...], tri, seg_ref, bd, gsum_ref[...],
                                                        gmean, hg_st_ref.at[g])

    def ret_chain(g):
        for r in range(tm // RR):
            src, dst = tile_rows(g, r, RR)
            oret_ref[g, dst, :] = yield from _ret_block(ret[src, :W], ret[src, W:2 * W], ret[src, 2 * W:3 * W],
                                                        ret[src, 3 * W:], cos_ref[dst, :], sin_ref[dst, :],
                                                        perm_ref[...], decay_ref, qdec_ref[...], kdec_ref[...],
                                                        sdec_ref[...], bd, gmean, ret_st_ref.at[g])

    def key_bias_chain(g):
        for r in range(tm // R):
            src, dst = tile_rows(g, r, R)
            kb_ref[g, dst, :] = yield from _key_bias_block(ff[src], tri, place_ref, carry_ref.at[g])

    def mem_chain(g):
        omem_ref[g] = yield from _mem_block(mq[g * tm:(g + 1) * tm], kv_ref[g, :, :W], kv_ref[g, :, W:])

    _run_interleaved([chain(g) for chain in (hgrn_chain, ret_chain, key_bias_chain, mem_chain) for g in range(G)])


def _mixer(x3, wt_mix, b_mix, mem, w_kv, lb_row, ng_row, gb, tm, ret_rows, hgrn_rows):
    B, S, D = x3.shape
    M = mem.shape[1]
    cos, sin, perm = _rope_tables(S)
    decay, qdec, kdec, sdec = _ret_tables(ret_rows)
    bd = _block_diag(jnp.ones((N_HEADS,), F32))
    gsum = bd.astype(BF16)
    gmean = (bd / HEAD_DIM).astype(BF16)
    idx = jnp.arange(hgrn_rows)
    tri = (idx[:, None] >= idx[None, :]).astype(F32)
    seg = jnp.stack([(idx[:, None] // (2 * m) == idx[None, :] // (2 * m)).astype(F32)
                     for m in _hgrn_levels(hgrn_rows)])
    place = _fox_place_table()
    const = lambda a: pl.BlockSpec(a.shape, lambda b, j: (0,) * a.ndim)
    tile = lambda w: pl.BlockSpec((gb, tm, w), lambda b, j: (b, j, 0))
    pos_tab = pl.BlockSpec((tm, BRANCH_W), lambda b, j: (j, 0))
    return pl.pallas_call(
        _mixer_kernel,
        out_shape=[jax.ShapeDtypeStruct((B, S, _C_FOXF), BF16)] + [jax.ShapeDtypeStruct((B, S, BRANCH_W), BF16)] * 4,
        grid=(B // gb, S // tm),
        in_specs=[tile(D), const(wt_mix), const(b_mix),
                  pl.BlockSpec((gb, M, D), lambda b, j: (b, 0, 0)), const(w_kv),
                  pos_tab, pos_tab, const(perm), const(decay), const(qdec), const(kdec), const(sdec),
                  const(bd), const(gsum), const(gmean), const(lb_row), const(ng_row), const(tri), const(seg),
                  const(place)],
        out_specs=[tile(_C_FOXF)] + [tile(BRANCH_W)] * 4,
        scratch_shapes=[pltpu.VMEM((gb, M, 2 * BRANCH_W), BF16),
                        pltpu.VMEM((gb, BRANCH_W, BRANCH_W), F32),
                        pltpu.VMEM((gb, BRANCH_W, BRANCH_W), F32),
                        pltpu.VMEM((gb, 8, LANES), F32)],
        compiler_params=_cparams(("parallel", "arbitrary")),
        name="mixer",
    )(x3, wt_mix, b_mix, mem, w_kv, cos, sin, perm, decay, qdec, kdec, sdec, bd, gsum, gmean, lb_row, ng_row,
      tri, seg, place)


_R_W1, _R_W2, _R_E1, _R_E2, _R_CLASS, _R_RANK = range(6)


def _route_block(x1, wr_hi_ref, wr_lo_ref, br_ref, lstrict_ref, run_ref):
    tm = x1.shape[0]
    x_hi = x1.astype(BF16)
    x_lo = (x1 - x_hi.astype(F32)).astype(BF16)
    logits = (_dot(x_hi, wr_hi_ref[...]) + _dot(x_hi, wr_lo_ref[...]) + _dot(x_lo, wr_hi_ref[...])) + br_ref[...]
    lane = lax.broadcasted_iota(I32, (tm, LANES), 1).astype(F32)
    group = jnp.floor(lane * (1.0 / EXPERTS_PER_GROUP))
    e = jnp.exp(logits - jnp.max(logits, axis=-1, keepdims=True))
    p = e / jnp.sum(e, axis=-1, keepdims=True)
    p1 = jnp.max(p, axis=-1, keepdims=True)
    e1 = jnp.min(jnp.where(p == p1, lane, float(LANES)), axis=-1, keepdims=True)
    in_group = group == jnp.floor(e1 * (1.0 / EXPERTS_PER_GROUP))
    rest = jnp.where(in_group, jnp.where(lane == e1, -1.0, p), -1.0)
    p2 = jnp.max(rest, axis=-1, keepdims=True)
    e2 = jnp.min(jnp.where(rest == p2, lane, float(LANES)), axis=-1, keepdims=True)
    w1 = p1 / (p1 + p2)
    w2 = p2 / (p1 + p2)

    grp = jnp.floor(e1 * (1.0 / EXPERTS_PER_GROUP))
    lo = jnp.minimum(e1, e2) - EXPERTS_PER_GROUP * grp
    hi = jnp.maximum(e1, e2) - EXPERTS_PER_GROUP * grp
    cls = grp * PAIRS_PER_GROUP + lo * (EXPERTS_PER_GROUP - 1) - lo * (lo - 1.0) * 0.5 + (hi - lo - 1.0)
    sel = jnp.where(lane == cls, 1.0, 0.0)
    before = _dot(lstrict_ref[...], sel.astype(BF16)) + run_ref[0:1, :]
    rank = jnp.sum(jnp.where(lane == cls, before, 0.0), axis=-1, keepdims=True)
    run_ref[...] = run_ref[...] + jnp.sum(sel, axis=0, keepdims=True)

    info = jnp.zeros((tm, LANES), F32)
    for col, val in ((_R_W1, w1), (_R_W2, w2), (_R_E1, e1), (_R_E2, e2), (_R_CLASS, cls), (_R_RANK, rank)):
        info = jnp.where(lane == float(col), val, info)
    pick = (lax.broadcasted_iota(I32, (8, LANES), 0) == lax.broadcasted_iota(I32, (8, LANES), 1)).astype(F32)
    rows = lax.dot_general(pick, info, _NT, precision=HIGHEST, preferred_element_type=F32)
    return info, rows


def _merge_kernel(x_ref, b0_ref, b1_ref, b2_ref, b3_ref, wgt_ref, bg_ref, wb_ref, wo_ref, lng_ref, lnb_ref,
                  wr_hi_ref, wr_lo_ref, br_ref, lstrict_ref, o_ref, info_ref, rows_ref, cnt_ref, run_ref, *, alpha):
    @pl.when(pl.program_id(0) == 0)
    def _():
        run_ref[...] = jnp.zeros(run_ref.shape, F32)

    x = x_ref[...]
    xb = x.astype(BF16)
    D = x.shape[1]
    mixed = jnp.zeros(x.shape, F32)
    for n, br in enumerate((b0_ref, b1_ref, b2_ref, b3_ref)):
        gate = jax.nn.sigmoid(_dot_nt(xb, wgt_ref[n * D:(n + 1) * D, :]) + bg_ref[:, n * D:(n + 1) * D])
        mixed = mixed + gate * _dot(br[...], wb_ref[n])
    z = alpha * x + _dot(mixed.astype(BF16), wo_ref[...])
    x1 = _layer_norm(z, lng_ref[...], lnb_ref[...])
    o_ref[...] = x1
    info, rows = _route_block(x1, wr_hi_ref, wr_lo_ref, br_ref, lstrict_ref, run_ref)
    info_ref[...] = info
    rows_ref[...] = rows
    cnt_ref[...] = run_ref[...]


def _merge(x2, branches, wgt, bg, wb, wo, lng, lnb, wr_pad, br_pad, alpha, tm):
    T, D = x2.shape
    const = lambda a: pl.BlockSpec(a.shape, lambda i: (0,) * a.ndim)
    br_spec = pl.BlockSpec((tm, BRANCH_W), lambda i: (i, 0))
    idx = jnp.arange(tm)
    lstrict = (idx[:, None] > idx[None, :]).astype(BF16)
    wr_hi = wr_pad.astype(BF16)
    wr_lo = (wr_pad - wr_hi.astype(F32)).astype(BF16)
    return pl.pallas_call(
        functools.partial(_merge_kernel, alpha=alpha),
        out_shape=[jax.ShapeDtypeStruct((T, D), F32), jax.ShapeDtypeStruct((T, LANES), F32),
                   jax.ShapeDtypeStruct((8, T), F32), jax.ShapeDtypeStruct((8, LANES), F32)],
        grid=(T // tm,),
        in_specs=[pl.BlockSpec((tm, D), lambda i: (i, 0)), br_spec, br_spec, br_spec, br_spec,
                  const(wgt), const(bg), const(wb), const(wo), const(lng), const(lnb),
                  const(wr_hi), const(wr_lo), const(br_pad), const(lstrict)],
        out_specs=[pl.BlockSpec((tm, D), lambda i: (i, 0)),
                   pl.BlockSpec((tm, LANES), lambda i: (i, 0)),
                   pl.BlockSpec((8, tm), lambda i: (0, i)),
                   pl.BlockSpec((8, LANES), lambda i: (0, 0))],
        scratch_shapes=[pltpu.VMEM((8, LANES), F32)],
        compiler_params=_cparams(("arbitrary",)),
        name="merge_ln_route",
    )(x2, *branches, wgt, bg, wb, wo, lng, lnb, wr_hi, wr_lo, br_pad, lstrict)


def _pos_kernel(off_ref, rows_ref, pos_ref):
    rows = rows_ref[...]
    cls = rows[_R_CLASS:_R_CLASS + 1, :]
    start = jnp.zeros(cls.shape, F32)
    for c in range(N_CLASSES):
        start = jnp.where(cls == float(c), off_ref[c].astype(F32), start)
    pos_ref[...] = jnp.zeros(pos_ref.shape, I32)
    pos_ref[0:1, :] = (rows[_R_RANK:_R_RANK + 1, :] + start).astype(I32)


def _positions(offsets, rows, tl):
    T = rows.shape[1]
    return pl.pallas_call(
        _pos_kernel,
        out_shape=jax.ShapeDtypeStruct((8, T), I32),
        grid_spec=pltpu.PrefetchScalarGridSpec(
            num_scalar_prefetch=1, grid=(T // tl,),
            in_specs=[pl.BlockSpec((8, tl), lambda i, off: (0, i))],
            out_specs=pl.BlockSpec((8, tl), lambda i, off: (0, i))),
        compiler_params=_cparams(("parallel",)),
        name="moe_positions",
    )(offsets, rows)


def _from_slab(ref, idx, rows, slab):
    return jnp.concatenate([ref[idx + (pl.ds(s, rows, stride=slab), slice(None))] for s in range(slab)], axis=1)


def _dispatch_kernel(pad_lo_ref, pad_hi_ref, nu_ref, pos_ref, x_ref, xs_ref, slab_ref, zero_ref, sem, *, bm):
    td, D = x_ref.shape
    slab = D // LANES
    i = pl.program_id(0)
    slot = i % 2

    def wait_slot(s):
        pltpu.make_async_copy(slab_ref.at[s], xs_ref.at[pl.ds(0, td * slab)], sem.at[s]).wait()

    @pl.when(i >= 2)
    def _():
        wait_slot(slot)

    x = x_ref[...]
    for s in range(slab):
        slab_ref[slot, pl.ds(s, td, stride=slab), :] = x[:, s * LANES:(s + 1) * LANES]

    def issue(t, carry):
        src = slab_ref.at[slot, pl.ds(pl.multiple_of(t * slab, slab), slab)]
        dst = xs_ref.at[pl.ds(pl.multiple_of(pos_ref[t] * slab, slab), slab)]
        pltpu.make_async_copy(src, dst, sem.at[slot]).start()
        return carry

    lax.fori_loop(0, td, issue, 0, unroll=DMA_ISSUE_UNROLL)

    last = pl.num_programs(0) - 1

    @pl.when(i == last)
    def _():
        wait_slot(slot)

    @pl.when(jnp.logical_and(i == last, i >= 1))
    def _():
        wait_slot(1 - slot)

    @pl.when(i == last)
    def _():
        zero_ref[...] = jnp.zeros(zero_ref.shape, F32)
        zero_row = zero_ref.at[pl.ds(0, slab)]
        n_blocks = xs_ref.shape[0] // (bm * slab)

        def row_copy(r):
            return pltpu.make_async_copy(zero_row, xs_ref.at[pl.ds(pl.multiple_of(r * slab, slab), slab)], sem.at[2])

        def block_copy(blk):
            start = pl.multiple_of(blk * (bm * slab), bm * slab)
            return pltpu.make_async_copy(zero_ref, xs_ref.at[pl.ds(start, bm * slab)], sem.at[2])

        def start_row(r, carry):
            row_copy(r).start()
            return carry

        def wait_row(r, carry):
            row_copy(r).wait()
            return carry

        def start_block(blk, carry):
            block_copy(blk).start()
            return carry

        def wait_block(blk, carry):
            block_copy(blk).wait()
            return carry

        for c in range(N_CLASSES):
            lax.fori_loop(pad_lo_ref[c], pad_hi_ref[c], start_row, 0)
        lax.fori_loop(nu_ref[0], n_blocks, start_block, 0)
        for c in range(N_CLASSES):
            lax.fori_loop(pad_lo_ref[c], pad_hi_ref[c], wait_row, 0)
        lax.fori_loop(nu_ref[0], n_blocks, wait_block, 0)


def _dispatch(pad_lo, pad_hi, n_used, pos, x2, n_slots, td, bm):
    T, D = x2.shape
    slab = D // LANES
    smem_blk = pl.BlockSpec((td,), lambda i, lo, hi, nu: (i,), memory_space=pltpu.SMEM)
    return pl.pallas_call(
        functools.partial(_dispatch_kernel, bm=bm),
        out_shape=jax.ShapeDtypeStruct((n_slots * slab, LANES), F32),
        grid_spec=pltpu.PrefetchScalarGridSpec(
            num_scalar_prefetch=3, grid=(T // td,),
            in_specs=[smem_blk, pl.BlockSpec((td, D), lambda i, lo, hi, nu: (i, 0))],
            out_specs=pl.BlockSpec(memory_space=pl.ANY),
            scratch_shapes=[pltpu.VMEM((2, td * slab, LANES), F32), pltpu.VMEM((bm * slab, LANES), F32),
                            pltpu.SemaphoreType.DMA((3,))]),
        compiler_params=_cparams(("arbitrary",)),
        name="moe_dispatch",
    )(pad_lo, pad_hi, n_used, pos, x2)


def _expert_kernel(ea_ref, eb_ref, nu_ref, xs_ref, wia_ref, woa_ref, wib_ref, wob_ref, ys_ref):
    del ea_ref, eb_ref
    used = pl.program_id(0) < nu_ref[0]

    @pl.when(used)
    def _():
        F = woa_ref.shape[1]
        slab = woa_ref.shape[2] // LANES
        bm = xs_ref.shape[0] // slab
        xb = _from_slab(xs_ref, (), bm, slab).astype(BF16)
        for half, (wi_ref, wo_ref) in enumerate(((wia_ref, woa_ref), (wib_ref, wob_ref))):
            h = _dot(xb, wi_ref[0])
            act = _silu(h[:, :F]) * h[:, F:]
            y = _dot(act.astype(BF16), wo_ref[0])
            for s in range(slab):
                ys_ref[pl.ds(half * slab + s, bm, stride=2 * slab), :] = y[:, s * LANES:(s + 1) * LANES]

    @pl.when(jnp.logical_not(used))
    def _():
        ys_ref[...] = jnp.zeros(ys_ref.shape, F32)


def _experts(block_a, block_b, n_used, xs, w_in, w_out, bm):
    E, D, F2 = w_in.shape
    slab = D // LANES
    n_blocks = xs.shape[0] // (bm * slab)
    row_blk = lambda i, ea, eb, nu: (jnp.maximum(jnp.minimum(i, nu[0] - 1), 0), 0)
    w_in_of = lambda table: pl.BlockSpec((1, D, F2), lambda i, ea, eb, nu: ((ea, eb)[table][i], 0, 0))
    w_out_of = lambda table: pl.BlockSpec((1, F2 // 2, D), lambda i, ea, eb, nu: ((ea, eb)[table][i], 0, 0))
    return pl.pallas_call(
        _expert_kernel,
        out_shape=jax.ShapeDtypeStruct((2 * xs.shape[0], LANES), F32),
        grid_spec=pltpu.PrefetchScalarGridSpec(
            num_scalar_prefetch=3, grid=(n_blocks,),
            in_specs=[pl.BlockSpec((bm * slab, LANES), row_blk), w_in_of(0), w_out_of(0), w_in_of(1), w_out_of(1)],
            out_specs=pl.BlockSpec((2 * bm * slab, LANES), lambda i, ea, eb, nu: (i, 0))),
        compiler_params=_cparams(("arbitrary",)),
        name="moe_experts",
    )(block_a, block_b, n_used, xs, w_in, w_out, w_in, w_out)


def _combine_kernel(pos_ref, nxt_ref, info_ref, x_ref, ys_ref, lng_ref, lnb_ref, o_ref, buf_ref, sem, *, alpha):
    tc, D = x_ref.shape
    slab2 = 2 * (D // LANES)
    i = pl.program_id(0)
    slot = i % 2

    def gather(p_ref, into):
        def issue(t, carry):
            src = ys_ref.at[pl.ds(pl.multiple_of(p_ref[t] * slab2, slab2), slab2)]
            dst = buf_ref.at[into, pl.ds(pl.multiple_of(t * slab2, slab2), slab2)]
            pltpu.make_async_copy(src, dst, sem.at[into]).start()
            return carry

        lax.fori_loop(0, tc, issue, 0, unroll=DMA_ISSUE_UNROLL)

    @pl.when(i == 0)
    def _():
        gather(pos_ref, slot)

    @pl.when(i + 1 < pl.num_programs(0))
    def _():
        gather(nxt_ref, 1 - slot)

    pltpu.make_async_copy(ys_ref.at[pl.ds(0, tc * slab2)], buf_ref.at[slot], sem.at[slot]).wait()
    info = info_ref[...]
    both = _from_slab(buf_ref, (slot,), tc, slab2)
    w1 = info[:, _R_W1:_R_W1 + 1]
    w2 = info[:, _R_W2:_R_W2 + 1]
    first_is_lower = info[:, _R_E1:_R_E1 + 1] < info[:, _R_E2:_R_E2 + 1]
    moe = jnp.where(first_is_lower, w1, w2) * both[:, :D] + jnp.where(first_is_lower, w2, w1) * both[:, D:]
    o_ref[...] = _layer_norm(alpha * x_ref[...] + moe, lng_ref[...], lnb_ref[...])


def _combine(pos, info, x2, ys, lng, lnb, alpha, tc):
    T, D = x2.shape
    n = T // tc
    cur = pl.BlockSpec((tc,), lambda i: (i,), memory_space=pltpu.SMEM)
    nxt = pl.BlockSpec((tc,), lambda i: (jnp.minimum(i + 1, n - 1),), memory_space=pltpu.SMEM)
    return pl.pallas_call(
        functools.partial(_combine_kernel, alpha=alpha),
        out_shape=jax.ShapeDtypeStruct((T, D), F32),
        grid=(n,),
        in_specs=[cur, nxt,
                  pl.BlockSpec((tc, LANES), lambda i: (i, 0)),
                  pl.BlockSpec((tc, D), lambda i: (i, 0)),
                  pl.BlockSpec(memory_space=pl.ANY),
                  pl.BlockSpec((1, D), lambda i: (0, 0)),
                  pl.BlockSpec((1, D), lambda i: (0, 0))],
        out_specs=pl.BlockSpec((tc, D), lambda i: (i, 0)),
        scratch_shapes=[pltpu.VMEM((2, 2 * tc * (D // LANES), LANES), F32), pltpu.SemaphoreType.DMA((2,))],
        compiler_params=_cparams(("arbitrary",)),
        name="moe_combine",
    )(pos, pos, info, x2, ys, lng, lnb)


def _moe(x2, info, rows, counts, w_e_in, w_e_out, lng, lnb, alpha, tiles):
    T, D = x2.shape
    bm = tiles["expert"]
    cnt = counts[0, :N_CLASSES].astype(I32)
    padded = ((cnt + bm - 1) // bm) * bm
    ends = jnp.cumsum(padded)
    offsets = ends - padded
    n_blocks = T // bm + N_CLASSES
    n_used = (ends[-1] // bm).astype(I32)
    blk_start = jnp.arange(n_blocks, dtype=I32) * bm
    block_class = jnp.sum((blk_start[:, None] >= ends[None, :]).astype(I32), axis=1)
    last_class = jnp.sum((blk_start[n_used - 1] >= ends).astype(I32))
    block_class = jnp.where(jnp.arange(n_blocks) < n_used, block_class, last_class)
    pairs = [(lo, hi) for lo in range(EXPERTS_PER_GROUP) for hi in range(lo + 1, EXPERTS_PER_GROUP)]
    groups = range(N_EXPERTS // EXPERTS_PER_GROUP)
    lower = jnp.asarray([g * EXPERTS_PER_GROUP + lo for g in groups for lo, _ in pairs], I32)
    higher = jnp.asarray([g * EXPERTS_PER_GROUP + hi for g in groups for _, hi in pairs], I32)

    pos = _positions(offsets.astype(I32), rows, tiles["pos"])[0]
    n_used = n_used.reshape(1)
    xs = _dispatch((offsets + cnt).astype(I32), ends.astype(I32), n_used, pos, x2, n_blocks * bm,
                   tiles["dispatch"], bm)
    ys = _experts(lower[block_class], higher[block_class], n_used, xs, w_e_in, w_e_out, bm)
    return _combine(pos, info, x2, ys, lng, lnb, alpha, tiles["combine"])


def _tiles(B, S):
    T = B * S
    pick = lambda want, n: math.gcd(want, n)
    return dict(mixer_batch=pick(MIXER_BATCH, B), mixer=pick(MIXER_ROWS, S), fox=pick(FOX_BLOCK, S), ret=pick(RET_ROWS, S), hgrn=pick(HGRN_ROWS, S),
                merge=pick(MERGE_ROWS, T),
                pos=pick(POS_LANES, T), dispatch=pick(DISPATCH_ROWS, T), expert=pick(EXPERT_ROWS, T),
                combine=pick(COMBINE_ROWS, T))


def _pack_input_weights(w_in, b_in):
    c = 3 * BRANCH_W
    n_mix = c + N_HEADS + 9 * BRANCH_W
    col_scale = jnp.ones((w_in.shape[-1],), F32).at[:BRANCH_W].set(HEAD_DIM ** -0.5 * LOG2E)
    wt = jnp.transpose(w_in * col_scale, (2, 0, 1))
    b = b_in * col_scale
    packed = []
    for l in range(w_in.shape[0]):
        w = wt[:, l, :]
        wt_mix = jnp.concatenate([w[:c + N_HEADS].astype(BF16), jnp.zeros((LANES - N_HEADS, w.shape[1]), BF16),
                                  w[c + N_HEADS:n_mix].astype(BF16)], axis=0)
        b_mix = jnp.concatenate([b[l, :c + N_HEADS], jnp.zeros((LANES - N_HEADS,), F32), b[l, c + N_HEADS:n_mix]])
        packed.append((wt_mix, b_mix[None, :], w[n_mix:].astype(BF16), b[l][None, n_mix:]))
    return packed


def kernel(x, mem, w_in, b_in, w_mem_kv, hgrn_lb, hgrn_norm_g, w_branch, w_out, ln_g, ln_b, w_router, b_router,
           w_e_in, w_e_out):
    B, S, D = x.shape
    T = B * S
    depth = w_in.shape[0]
    alpha = (2.0 * depth) ** 0.25
    tiles = _tiles(B, S)

    lb_all = jax.nn.softmax(hgrn_lb.astype(F32), axis=0)
    lb_all = jnp.cumsum(lb_all, axis=0) - lb_all[0:1]
    wr_pad = jnp.concatenate([w_router, jnp.zeros((D, LANES - N_EXPERTS), F32)], axis=1)
    br_pad = jnp.concatenate([b_router, jnp.full((LANES - N_EXPERTS,), MASK_VALUE, F32)])[None, :]

    x2 = x.reshape(T, D)
    packed = _pack_input_weights(w_in, b_in)
    for l in range(depth):
        wt_mix, b_mix, wt_gate, b_gate = packed[l]
        qkv, kb, o_ret, o_hg, o_mem = _mixer(x2.reshape(B, S, D), wt_mix, b_mix, mem, w_mem_kv[l].astype(BF16),
                                             lb_all[l][None, :], jnp.tile(hgrn_norm_g[l], N_HEADS)[None, :],
                                             tiles["mixer_batch"], tiles["mixer"], tiles["ret"], tiles["hgrn"])
        o_fox = _fox_attention(qkv, kb, tiles["fox"])
        branches = [o.reshape(T, BRANCH_W) for o in (o_fox, o_ret, o_hg, o_mem)]
        x2, info, rows, counts = _merge(x2, branches, wt_gate, b_gate, w_branch[l].astype(BF16),
                                        w_out[l].astype(BF16), ln_g[l, 0][None, :], ln_b[l, 0][None, :],
                                        wr_pad, br_pad, alpha, tiles["merge"])
        x2 = _moe(x2, info, rows, counts, w_e_in[l].astype(BF16), w_e_out[l].astype(BF16),
                  ln_g[l, 1][None, :], ln_b[l, 1][None, :], alpha, tiles)
    return x2.reshape(B, S, D)
```

```python
import functools
import math

import jax
import jax.numpy as jnp
from jax import lax
from jax.experimental import pallas as pl
from jax.experimental.pallas import tpu as pltpu

F32 = jnp.float32
BF16 = jnp.bfloat16
I32 = jnp.int32
HIGHEST = lax.Precision.HIGHEST

N_HEADS = 4
HEAD_DIM = 64
BRANCH_W = N_HEADS * HEAD_DIM
N_BRANCH = 4
RET_CHUNK = 64
ROPE_BASE = 10000.0
N_EXPERTS = 16
EXPERTS_PER_GROUP = 4
PAIRS_PER_GROUP = EXPERTS_PER_GROUP * (EXPERTS_PER_GROUP - 1) // 2
N_CLASSES = (N_EXPERTS // EXPERTS_PER_GROUP) * PAIRS_PER_GROUP
LN_EPS = 1e-5
GN_EPS = 1e-6
MASK_VALUE = -1e30
LOG2E = math.log2(math.e)

LANES = 128
V7X_VMEM_LIMIT_BYTES = 52 * 1024 * 1024

MIXER_ROWS = 512
MIXER_BATCH = 2
FOX_BLOCK = 512
RET_ROWS = 256
HGRN_ROWS = 128
HGRN_SUB = 4
MERGE_ROWS = 512
POS_LANES = 2048
DISPATCH_ROWS = 512
EXPERT_ROWS = 512
COMBINE_ROWS = 256
DMA_ISSUE_UNROLL = 8

_W_FOX = 3 * BRANCH_W
_C_FOXF = _W_FOX
_C_RET = _C_FOXF + LANES
_C_HGF = _C_RET + 4 * BRANCH_W
_C_HGR = _C_HGF + BRANCH_W
_C_MQ = _C_HGR + 3 * BRANCH_W
_C_END = _C_MQ + BRANCH_W

_NT = (((1,), (1,)), ((), ()))


def _cparams(sem):
    return pltpu.CompilerParams(dimension_semantics=sem, vmem_limit_bytes=V7X_VMEM_LIMIT_BYTES)


def _head_id(shape):
    return lax.broadcasted_iota(I32, shape, len(shape) - 1) // HEAD_DIM


def _dot(a, b):
    return jnp.dot(a, b, preferred_element_type=F32)


def _dot_nt(a, b):
    return lax.dot_general(a, b, _NT, preferred_element_type=F32)


def _silu(x):
    return x * jax.nn.sigmoid(x)


def _layer_norm(z, g, b):
    mu = jnp.mean(z, axis=-1, keepdims=True)
    d = z - mu
    var = jnp.mean(d * d, axis=-1, keepdims=True)
    return d * lax.rsqrt(var + LN_EPS) * g + b


_FOX_BIAS_PARTS = 3


def _fox_spare_lane(h):
    return HEAD_DIM * ((h + 1) % N_HEADS)


def _fox_place_table():
    lane = jnp.arange(BRANCH_W)
    head = jnp.arange(LANES)
    dest = jnp.where(head < N_HEADS, HEAD_DIM * ((head + 1) % N_HEADS), -BRANCH_W)
    return jnp.stack([(lane[None, :] == dest[:, None] + part) for part in range(_FOX_BIAS_PARTS)]).astype(BF16)


def _key_bias_block(ff, tri, place_ref, carry_ref):
    ls = jax.nn.log_sigmoid(ff) * LOG2E
    cs = jnp.dot(tri, ls, precision=HIGHEST, preferred_element_type=F32) + carry_ref[0:1, :]
    carry_ref[...] = jnp.broadcast_to(cs[ff.shape[0] - 1:, :], carry_ref.shape)
    yield
    rest = -cs
    kb = jnp.zeros((ff.shape[0], BRANCH_W), F32)
    for part in range(_FOX_BIAS_PARTS):
        piece = rest.astype(BF16)
        rest = rest - piece.astype(F32)
        kb = kb + _dot(piece, place_ref[part])
    return kb.astype(BF16)


def _fox_lane_table():
    lane = jnp.arange(BRANCH_W)
    rows = []
    for h in range(N_HEADS):
        spare = _fox_spare_lane(h)
        bias = (lane >= spare) & (lane < spare + _FOX_BIAS_PARTS)
        rows += [lane // HEAD_DIM == h, bias, ~bias, lane == spare]
    return jnp.stack(rows).astype(F32)


def _fox_kernel(q_ref, k_ref, v_ref, kb_ref, lanes_ref, o_ref, qh_ref, m_ref, acc_ref):
    qi = pl.program_id(1)
    ki = pl.program_id(2)
    blk = q_ref.shape[1]

    def pattern(h, r):
        return lanes_ref[4 * h + r:4 * h + r + 1, :].astype(BF16)

    @pl.when(ki == 0)
    def _():
        m_ref[...] = jnp.full(m_ref.shape, MASK_VALUE, F32)
        acc_ref[...] = jnp.zeros(acc_ref.shape, F32)
        q = q_ref[0]
        for h in range(N_HEADS):
            qh_ref[h] = q * pattern(h, 0) + pattern(h, 1)

    def sweep(masked):
        k = k_ref[0]
        v = v_ref[0]
        kb = kb_ref[0]
        if masked:
            causal = lax.broadcasted_iota(I32, (blk, blk), 0) >= lax.broadcasted_iota(I32, (blk, blk), 1)
        def logits(h):
            return _dot_nt(qh_ref[h], k * pattern(h, 2) + kb * pattern(h, 1))

        def softmax_step(h, s):
            if masked:
                s = jnp.where(causal, s, MASK_VALUE)
            m_prev = m_ref[h]
            m_new = jnp.maximum(m_prev, jnp.max(s, axis=-1, keepdims=True))
            m_ref[h] = m_new
            p = jnp.exp2(s - jnp.concatenate([m_new] * (blk // LANES), axis=1))
            return p.astype(BF16), jnp.exp2(m_prev - m_new)

        def accumulate(h, p, alpha):
            ones_lane = pattern(h, 3)
            vh = v * (1 - ones_lane) + ones_lane
            acc_ref[h] = acc_ref[h] * jnp.concatenate([alpha] * (BRANCH_W // LANES), axis=1) + _dot(p, vh)

        s_next = logits(0)
        for h in range(N_HEADS):
            s = s_next
            if h + 1 < N_HEADS:
                s_next = logits(h + 1)
            p, alpha = softmax_step(h, s)
            accumulate(h, p, alpha)

    @pl.when(ki < qi)
    def _():
        sweep(False)

    @pl.when(ki == qi)
    def _():
        sweep(True)
        lane = lax.broadcasted_iota(I32, (blk, BRANCH_W), 1)
        out = jnp.zeros((blk, BRANCH_W), F32)
        for h in range(N_HEADS):
            acc = acc_ref[h]
            spare = _fox_spare_lane(h)
            out = jnp.where((lane // HEAD_DIM) == h, acc / acc[:, spare:spare + 1], out)
        o_ref[0] = out.astype(BF16)


def _fox_attention(qkv3, kb, blk):
    B, S, _ = qkv3.shape
    n = S // blk
    kv_blk = lambda c: pl.BlockSpec((1, blk, BRANCH_W), lambda b, qi, ki: (b, jnp.minimum(ki, qi), c))
    lanes = _fox_lane_table()
    return pl.pallas_call(
        _fox_kernel,
        out_shape=jax.ShapeDtypeStruct((B, S, BRANCH_W), BF16),
        grid=(B, n, n),
        in_specs=[pl.BlockSpec((1, blk, BRANCH_W), lambda b, qi, ki: (b, qi, 0)), kv_blk(1), kv_blk(2),
                  pl.BlockSpec((1, blk, BRANCH_W), lambda b, qi, ki: (b, jnp.minimum(ki, qi), 0)),
                  pl.BlockSpec(lanes.shape, lambda b, qi, ki: (0, 0))],
        out_specs=pl.BlockSpec((1, blk, BRANCH_W), lambda b, qi, ki: (b, qi, 0)),
        scratch_shapes=[pltpu.VMEM((N_HEADS, blk, BRANCH_W), BF16),
                        pltpu.VMEM((N_HEADS, blk, LANES), F32),
                        pltpu.VMEM((N_HEADS, blk, BRANCH_W), F32)],
        compiler_params=_cparams(("parallel", "parallel", "arbitrary")),
        name="fox_attention",
    )(qkv3, qkv3, qkv3, kb, lanes)


def _mem_block(q, mk, mv):
    rows = q.shape[0]
    q = q * (HEAD_DIM ** -0.5)
    hid = _head_id((rows, BRANCH_W))
    out = jnp.zeros((rows, BRANCH_W), F32)
    for h in range(N_HEADS):
        qh = jnp.where(hid == h, q, jnp.zeros_like(q))
        s = _dot_nt(qh, mk)
        p = jnp.exp(s - jnp.max(s, axis=-1, keepdims=True))
        pv = _dot(p.astype(BF16), mv)
        out = jnp.where(hid == h, pv / jnp.sum(p, axis=-1, keepdims=True), out)
        yield
    return out.astype(BF16)


def _block_diag(val_per_head):
    hid = jnp.arange(BRANCH_W) // HEAD_DIM
    same = hid[:, None] == hid[None, :]
    return jnp.where(same, jnp.asarray(val_per_head, F32)[hid][:, None], 0.0)


def _rope_tables(S):
    half = HEAD_DIM // 2
    inv = ROPE_BASE ** (-2.0 * jnp.arange(half, dtype=F32) / HEAD_DIM)
    ang = jnp.arange(S, dtype=F32)[:, None] * inv[None, :]
    cos = jnp.tile(jnp.cos(ang), (1, 2 * N_HEADS))
    sin = jnp.tile(jnp.sin(ang), (1, 2 * N_HEADS))
    d = jnp.arange(BRANCH_W)
    lo = (d % HEAD_DIM) < half
    perm = jnp.where(lo[None, :] & (d[:, None] == d[None, :] + half), -1.0, 0.0)
    perm = perm + jnp.where((~lo)[None, :] & (d[:, None] == d[None, :] - half), 1.0, 0.0)
    return cos, sin, perm.astype(BF16)


def _ret_tables(R):
    log_gamma = jnp.log1p(-jnp.exp2(-5.0 - jnp.arange(N_HEADS, dtype=F32)))
    idx = jnp.arange(R, dtype=F32)
    chunk = jnp.arange(R) // RET_CHUNK
    dist = jnp.abs(idx[:, None] - idx[None, :])
    visible = chunk[None, :] <= chunk[:, None]
    decay = jnp.where(visible[None], jnp.exp(log_gamma[:, None, None] * dist[None]), 0.0)
    lanes_lg = jnp.repeat(log_gamma, HEAD_DIM)[None, :]
    qdec = jnp.exp(lanes_lg * (idx[:, None] + 1.0))
    kdec = jnp.exp(lanes_lg * (R - 1.0 - idx[:, None]))
    sdec = _block_diag(jnp.exp(log_gamma * R))
    return decay, qdec, kdec, sdec


def _ret_block(qb, kb, v, g, cos, sin, perm, decay_ref, qdec, kdec, sdec, bd, gmean, st_ref):
    R = qb.shape[0]
    qr = qb.astype(F32) * cos + _dot(qb, perm) * sin
    kr = (kb.astype(F32) * cos + _dot(kb, perm) * sin) * (HEAD_DIM ** -0.5)
    qrb = qr.astype(BF16)
    krb = kr.astype(BF16)
    hid = _head_id((R, BRANCH_W))
    yield

    st = st_ref[...]
    o = _dot((qr * qdec).astype(BF16), st.astype(BF16))
    for h in range(N_HEADS):
        qh = jnp.where(hid == h, qrb, jnp.zeros_like(qrb))
        a = _dot_nt(qh, krb) * decay_ref[h]
        o = o + jnp.where(hid == h, _dot(a.astype(BF16), v), 0.0)
        yield
    kd_t = (kr * kdec).T.astype(BF16)
    st_ref[...] = st * sdec + _dot(kd_t, v) * bd
    yield

    mu = _dot(o.astype(BF16), gmean)
    d = o - mu
    var = _dot((d * d).astype(BF16), gmean)
    on = d * lax.rsqrt(var + GN_EPS)
    return (on * _silu(g.astype(F32))).astype(BF16)


def _hgrn_levels(R):
    levels = []
    m = HGRN_SUB
    while 2 * m <= R:
        levels.append(m)
        m *= 2
    return tuple(levels)


def _hgrn_block(zf, vb, qb, g, lb, ng, tri, seg_ref, bd, gsum, gmean, st_ref):
    R = zf.shape[0]
    W = BRANCH_W
    kf = (1.0 - lb) * jax.nn.sigmoid(-zf)
    logf = jnp.log1p(-kf)
    b = jnp.dot(tri, logf, precision=HIGHEST, preferred_element_type=F32)
    yield
    q = qb.astype(F32)
    v = vb.astype(F32)
    hid = _head_id((R, W))
    row = lax.broadcasted_iota(I32, (R, W), 0)

    o = _dot((q * kf).astype(BF16), gsum) * v
    sub = row % HGRN_SUB
    for d in range(1, HGRN_SUB):
        e = jnp.exp(b - pltpu.roll(b, d, 0))
        w = jnp.where(sub >= d, q * pltpu.roll(kf, d, 0) * e, 0.0)
        o = o + _dot(w.astype(BF16), gsum) * pltpu.roll(v, d, 0)
        yield

    a_heads = [jnp.zeros((R, R), F32) for _ in range(N_HEADS)]
    for li, m in enumerate(_hgrn_levels(R)):
        nseg = R // (2 * m)
        bnd = jnp.broadcast_to(b.reshape(nseg, 2 * m, W)[:, m - 1:m, :], (nseg, 2 * m, W)).reshape(R, W)
        second = (row % (2 * m)) >= m
        qm = jnp.where(second, q * jnp.exp(b - bnd), 0.0).astype(BF16)
        km = jnp.where(second, 0.0, kf * jnp.exp(bnd - b)).astype(BF16)
        for h in range(N_HEADS):
            qh = jnp.where(hid == h, qm, jnp.zeros_like(qm))
            pairs = _dot_nt(qh, km)
            a_heads[h] = a_heads[h] + (pairs if nseg == 1 else pairs * seg_ref[li])
        yield
    for h in range(N_HEADS):
        o = o + jnp.where(hid == h, _dot(a_heads[h].astype(BF16), vb), 0.0)

    yield
    st = st_ref[...]
    o = o + _dot_nt((q * jnp.exp(b)).astype(BF16), st.astype(BF16))
    b_last = b[R - 1:R, :]
    kd = (kf * jnp.exp(b_last - b)).astype(BF16)
    st_ref[...] = st * jnp.exp(b_last) + _dot(v.T.astype(BF16), kd) * bd

    yield
    ms = _dot((o * o).astype(BF16), gmean)
    on = o * lax.rsqrt(ms + GN_EPS) * ng
    return (on * _silu(g.astype(F32))).astype(BF16)


def _run_interleaved(gens):
    results = [None] * len(gens)
    live = list(range(len(gens)))
    while live:
        for i in list(live):
            try:
                next(gens[i])
            except StopIteration as done:
                results[i] = done.value
                live.remove(i)
    return results


def _mixer_kernel(x_ref, wt_ref, b_ref, mem_ref, wkv_ref, cos_ref, sin_ref, perm_ref, decay_ref, qdec_ref,
                  kdec_ref, sdec_ref, bd_ref, gsum_ref, gmean_ref, lb_ref, ng_ref, tri_ref, seg_ref, place_ref,
                  qkv_ref, kb_ref, oret_ref, ohg_ref, omem_ref,
                  kv_ref, ret_st_ref, hg_st_ref, carry_ref):
    G, tm, D = x_ref.shape

    @pl.when(pl.program_id(1) == 0)
    def _():
        for g in range(G):
            kv_ref[g] = _dot(mem_ref[g].astype(BF16), wkv_ref[...]).astype(BF16)
        ret_st_ref[...] = jnp.zeros(ret_st_ref.shape, F32)
        hg_st_ref[...] = jnp.zeros(hg_st_ref.shape, F32)
        carry_ref[...] = jnp.zeros(carry_ref.shape, F32)

    xb = x_ref[...].reshape(G * tm, D).astype(BF16)
    W = BRANCH_W

    def seg(lo, hi):
        return _dot_nt(xb, wt_ref[lo:hi, :]) + b_ref[:, lo:hi]

    def tile_rows(g, r, n):
        return slice(g * tm + r * n, g * tm + (r + 1) * n), slice(r * n, (r + 1) * n)

    hgf = seg(_C_HGF, _C_HGR)
    hgr = seg(_C_HGR, _C_MQ).astype(BF16)
    ff = seg(_C_FOXF, _C_RET)
    ret = seg(_C_RET, _C_HGF).astype(BF16)
    mq = seg(_C_MQ, _C_END).astype(BF16)
    qkv = seg(0, _C_FOXF).astype(BF16)
    for g in range(G):
        qkv_ref[g] = qkv[g * tm:(g + 1) * tm]
    bd = bd_ref[...]
    gmean = gmean_ref[...]
    tri = tri_ref[...]
    R = tri.shape[0]
    RR = qdec_ref.shape[0]

    def hgrn_chain(g):
        for r in range(tm // R):
            src, dst = tile_rows(g, r, R)
            ohg_ref[g, dst, :] = yield from _hgrn_block(hgf[src], hgr[src, :W], hgr[src, W:2 * W], hgr[src, 2 * W:],
                                                        lb_ref[...], ng_ref[...], tri, seg_ref, bd, gsum_ref[...],
                                                        gmean, hg_st_ref.at[g])

    def ret_chain(g):
        for r in range(tm // RR):
            src, dst = tile_rows(g, r, RR)
            oret_ref[g, dst, :] = yield from _ret_block(ret[src, :W], ret[src, W:2 * W], ret[src, 2 * W:3 * W],
                                                        ret[src, 3 * W:], cos_ref[dst, :], sin_ref[dst, :],
                                                        perm_ref[...], decay_ref, qdec_ref[...], kdec_ref[...],
                                                        sdec_ref[...], bd, gmean, ret_st_ref.at[g])

    def key_bias_chain(g):
        for r in range(tm // R):
            src, dst = tile_rows(g, r, R)
            kb_ref[g, dst, :] = yield from _key_bias_block(ff[src], tri, place_ref, carry_ref.at[g])

    def mem_chain(g):
        omem_ref[g] = yield from _mem_block(mq[g * tm:(g + 1) * tm], kv_ref[g, :, :W], kv_ref[g, :, W:])

    _run_interleaved([chain(g) for chain in (hgrn_chain, ret_chain, key_bias_chain, mem_chain) for g in range(G)])


def _mixer(x3, wt_mix, b_mix, mem, w_kv, lb_row, ng_row, gb, tm, ret_rows, hgrn_rows):
    B, S, D = x3.shape
    M = mem.shape[1]
    cos, sin, perm = _rope_tables(S)
    decay, qdec, kdec, sdec = _ret_tables(ret_rows)
    bd = _block_diag(jnp.ones((N_HEADS,), F32))
    gsum = bd.astype(BF16)
    gmean = (bd / HEAD_DIM).astype(BF16)
    idx = jnp.arange(hgrn_rows)
    tri = (idx[:, None] >= idx[None, :]).astype(F32)
    seg = jnp.stack([(idx[:, None] // (2 * m) == idx[None, :] // (2 * m)).astype(F32)
                     for m in _hgrn_levels(hgrn_rows)])
    place = _fox_place_table()
    const = lambda a: pl.BlockSpec(a.shape, lambda b, j: (0,) * a.ndim)
    tile = lambda w: pl.BlockSpec((gb, tm, w), lambda b, j: (b, j, 0))
    pos_tab = pl.BlockSpec((tm, BRANCH_W), lambda b, j: (j, 0))
    return pl.pallas_call(
        _mixer_kernel,
        out_shape=[jax.ShapeDtypeStruct((B, S, _C_FOXF), BF16)] + [jax.ShapeDtypeStruct((B, S, BRANCH_W), BF16)] * 4,
        grid=(B // gb, S // tm),
        in_specs=[tile(D), const(wt_mix), const(b_mix),
                  pl.BlockSpec((gb, M, D), lambda b, j: (b, 0, 0)), const(w_kv),
                  pos_tab, pos_tab, const(perm), const(decay), const(qdec), const(kdec), const(sdec),
                  const(bd), const(gsum), const(gmean), const(lb_row), const(ng_row), const(tri), const(seg),
                  const(place)],
        out_specs=[tile(_C_FOXF)] + [tile(BRANCH_W)] * 4,
        scratch_shapes=[pltpu.VMEM((gb, M, 2 * BRANCH_W), BF16),
                        pltpu.VMEM((gb, BRANCH_W, BRANCH_W), F32),
                        pltpu.VMEM((gb, BRANCH_W, BRANCH_W), F32),
                        pltpu.VMEM((gb, 8, LANES), F32)],
        compiler_params=_cparams(("parallel", "arbitrary")),
        name="mixer",
    )(x3, wt_mix, b_mix, mem, w_kv, cos, sin, perm, decay, qdec, kdec, sdec, bd, gsum, gmean, lb_row, ng_row,
      tri, seg, place)


_R_W1, _R_W2, _R_E1, _R_E2, _R_CLASS, _R_RANK = range(6)


def _route_block(x1, wr_hi_ref, wr_lo_ref, br_ref, lstrict_ref, run_ref, counted):
    tm = x1.shape[0]
    x_hi = x1.astype(BF16)
    x_lo = (x1 - x_hi.astype(F32)).astype(BF16)
    logits = (_dot(x_hi, wr_hi_ref[...]) + _dot(x_hi, wr_lo_ref[...]) + _dot(x_lo, wr_hi_ref[...])) + br_ref[...]
    yield
    lane = lax.broadcasted_iota(I32, (tm, LANES), 1).astype(F32)
    group = jnp.floor(lane * (1.0 / EXPERTS_PER_GROUP))
    e = jnp.exp(logits - jnp.max(logits, axis=-1, keepdims=True))
    p = e / jnp.sum(e, axis=-1, keepdims=True)
    p1 = jnp.max(p, axis=-1, keepdims=True)
    e1 = jnp.min(jnp.where(p == p1, lane, float(LANES)), axis=-1, keepdims=True)
    in_group = group == jnp.floor(e1 * (1.0 / EXPERTS_PER_GROUP))
    rest = jnp.where(in_group, jnp.where(lane == e1, -1.0, p), -1.0)
    p2 = jnp.max(rest, axis=-1, keepdims=True)
    e2 = jnp.min(jnp.where(rest == p2, lane, float(LANES)), axis=-1, keepdims=True)
    w1 = p1 / (p1 + p2)
    w2 = p2 / (p1 + p2)
    yield

    grp = jnp.floor(e1 * (1.0 / EXPERTS_PER_GROUP))
    lo = jnp.minimum(e1, e2) - EXPERTS_PER_GROUP * grp
    hi = jnp.maximum(e1, e2) - EXPERTS_PER_GROUP * grp
    cls = grp * PAIRS_PER_GROUP + lo * (EXPERTS_PER_GROUP - 1) - lo * (lo - 1.0) * 0.5 + (hi - lo - 1.0)
    sel = jnp.where(lane == cls, 1.0, 0.0)
    before = _dot(lstrict_ref[...], sel.astype(BF16)) + run_ref[0:1, :]
    rank = jnp.sum(jnp.where(lane == cls, before, 0.0), axis=-1, keepdims=True)
    run_ref[...] = run_ref[...] + counted * jnp.sum(sel, axis=0, keepdims=True)
    yield

    info = jnp.zeros((tm, LANES), F32)
    for col, val in ((_R_W1, w1), (_R_W2, w2), (_R_E1, e1), (_R_E2, e2), (_R_CLASS, cls), (_R_RANK, rank)):
        info = jnp.where(lane == float(col), val, info)
    pick = (lax.broadcasted_iota(I32, (8, LANES), 0) == lax.broadcasted_iota(I32, (8, LANES), 1)).astype(F32)
    rows = lax.dot_general(pick, info, _NT, precision=HIGHEST, preferred_element_type=F32)
    return info, rows


def _merge_kernel(x_ref, b0_ref, b1_ref, b2_ref, b3_ref, wgt_ref, bg_ref, wb_ref, wo_ref, lng_ref, lnb_ref,
                  wr_hi_ref, wr_lo_ref, br_ref, lstrict_ref, o_ref, info_ref, rows_ref, cnt_ref, run_ref, x1_ref,
                  *, alpha):
    i = pl.program_id(0)

    @pl.when(i == 0)
    def _():
        run_ref[...] = jnp.zeros(run_ref.shape, F32)
        x1_ref[...] = jnp.zeros(x1_ref.shape, F32)

    x = x_ref[...]
    xb = x.astype(BF16)
    D = x.shape[1]

    def route_chain():
        info, rows = yield from _route_block(x1_ref[...], wr_hi_ref, wr_lo_ref, br_ref, lstrict_ref, run_ref,
                                             jnp.where(i > 0, 1.0, 0.0))
        info_ref[...] = info
        rows_ref[...] = rows
        cnt_ref[...] = run_ref[...]

    def merge_chain():
        mixed = jnp.zeros(x.shape, F32)
        for n, br in enumerate((b0_ref, b1_ref, b2_ref, b3_ref)):
            gate = jax.nn.sigmoid(_dot_nt(xb, wgt_ref[n * D:(n + 1) * D, :]) + bg_ref[:, n * D:(n + 1) * D])
            mixed = mixed + gate * _dot(br[...], wb_ref[n])
            yield
        return alpha * x + _dot(mixed.astype(BF16), wo_ref[...])

    _, z = _run_interleaved([route_chain(), merge_chain()])
    x1 = _layer_norm(z, lng_ref[...], lnb_ref[...])
    o_ref[...] = x1
    x1_ref[...] = x1


def _merge(x2, branches, wgt, bg, wb, wo, lng, lnb, wr_pad, br_pad, alpha, tm):
    T, D = x2.shape
    n = T // tm
    const = lambda a: pl.BlockSpec(a.shape, lambda i: (0,) * a.ndim)
    merged = lambda i: (jnp.minimum(i, n - 1), 0)
    routed = lambda i: jnp.maximum(i - 1, 0)
    br_spec = pl.BlockSpec((tm, BRANCH_W), merged)
    idx = jnp.arange(tm)
    lstrict = (idx[:, None] > idx[None, :]).astype(BF16)
    wr_hi = wr_pad.astype(BF16)
    wr_lo = (wr_pad - wr_hi.astype(F32)).astype(BF16)
    return pl.pallas_call(
        functools.partial(_merge_kernel, alpha=alpha),
        out_shape=[jax.ShapeDtypeStruct((T, D), F32), jax.ShapeDtypeStruct((T, LANES), F32),
                   jax.ShapeDtypeStruct((8, T), F32), jax.ShapeDtypeStruct((8, LANES), F32)],
        grid=(n + 1,),
        in_specs=[pl.BlockSpec((tm, D), merged), br_spec, br_spec, br_spec, br_spec,
                  const(wgt), const(bg), const(wb), const(wo), const(lng), const(lnb),
                  const(wr_hi), const(wr_lo), const(br_pad), const(lstrict)],
        out_specs=[pl.BlockSpec((tm, D), merged),
                   pl.BlockSpec((tm, LANES), lambda i: (routed(i), 0)),
                   pl.BlockSpec((8, tm), lambda i: (0, routed(i))),
                   pl.BlockSpec((8, LANES), lambda i: (0, 0))],
        scratch_shapes=[pltpu.VMEM((8, LANES), F32), pltpu.VMEM((tm, D), F32)],
        compiler_params=_cparams(("arbitrary",)),
        name="merge_ln_route",
    )(x2, *branches, wgt, bg, wb, wo, lng, lnb, wr_hi, wr_lo, br_pad, lstrict)


def _pos_kernel(off_ref, rows_ref, pos_ref):
    rows = rows_ref[...]
    cls = rows[_R_CLASS:_R_CLASS + 1, :]
    start = jnp.zeros(cls.shape, F32)
    for c in range(N_CLASSES):
        start = jnp.where(cls == float(c), off_ref[c].astype(F32), start)
    pos_ref[...] = jnp.zeros(pos_ref.shape, I32)
    pos_ref[0:1, :] = (rows[_R_RANK:_R_RANK + 1, :] + start).astype(I32)


def _positions(offsets, rows, tl):
    T = rows.shape[1]
    return pl.pallas_call(
        _pos_kernel,
        out_shape=jax.ShapeDtypeStruct((8, T), I32),
        grid_spec=pltpu.PrefetchScalarGridSpec(
            num_scalar_prefetch=1, grid=(T // tl,),
            in_specs=[pl.BlockSpec((8, tl), lambda i, off: (0, i))],
            out_specs=pl.BlockSpec((8, tl), lambda i, off: (0, i))),
        compiler_params=_cparams(("parallel",)),
        name="moe_positions",
    )(offsets, rows)


def _from_slab(ref, idx, rows, slab):
    return jnp.concatenate([ref[idx + (pl.ds(s, rows, stride=slab), slice(None))] for s in range(slab)], axis=1)


def _dispatch_kernel(pad_lo_ref, pad_hi_ref, nu_ref, pos_ref, x_ref, xs_ref, slab_ref, zero_ref, sem, *, bm):
    td, D = x_ref.shape
    slab = D // LANES
    i = pl.program_id(0)
    slot = i % 2

    def wait_slot(s):
        pltpu.make_async_copy(slab_ref.at[s], xs_ref.at[pl.ds(0, td * slab)], sem.at[s]).wait()

    @pl.when(i >= 2)
    def _():
        wait_slot(slot)

    x = x_ref[...]
    for s in range(slab):
        slab_ref[slot, pl.ds(s, td, stride=slab), :] = x[:, s * LANES:(s + 1) * LANES]

    def issue(t, carry):
        src = slab_ref.at[slot, pl.ds(pl.multiple_of(t * slab, slab), slab)]
        dst = xs_ref.at[pl.ds(pl.multiple_of(pos_ref[t] * slab, slab), slab)]
        pltpu.make_async_copy(src, dst, sem.at[slot]).start()
        return carry

    lax.fori_loop(0, td, issue, 0, unroll=DMA_ISSUE_UNROLL)

    last = pl.num_programs(0) - 1

    @pl.when(i == last)
    def _():
        wait_slot(slot)

    @pl.when(jnp.logical_and(i == last, i >= 1))
    def _():
        wait_slot(1 - slot)

    @pl.when(i == last)
    def _():
        zero_ref[...] = jnp.zeros(zero_ref.shape, F32)
        zero_row = zero_ref.at[pl.ds(0, slab)]
        n_blocks = xs_ref.shape[0] // (bm * slab)

        def row_copy(r):
            return pltpu.make_async_copy(zero_row, xs_ref.at[pl.ds(pl.multiple_of(r * slab, slab), slab)], sem.at[2])

        def block_copy(blk):
            start = pl.multiple_of(blk * (bm * slab), bm * slab)
            return pltpu.make_async_copy(zero_ref, xs_ref.at[pl.ds(start, bm * slab)], sem.at[2])

        def start_row(r, carry):
            row_copy(r).start()
            return carry

        def wait_row(r, carry):
            row_copy(r).wait()
            return carry

        def start_block(blk, carry):
            block_copy(blk).start()
            return carry

        def wait_block(blk, carry):
            block_copy(blk).wait()
            return carry

        for c in range(N_CLASSES):
            lax.fori_loop(pad_lo_ref[c], pad_hi_ref[c], start_row, 0)
        lax.fori_loop(nu_ref[0], n_blocks, start_block, 0)
        for c in range(N_CLASSES):
            lax.fori_loop(pad_lo_ref[c], pad_hi_ref[c], wait_row, 0)
        lax.fori_loop(nu_ref[0], n_blocks, wait_block, 0)


def _dispatch(pad_lo, pad_hi, n_used, pos, x2, n_slots, td, bm):
    T, D = x2.shape
    slab = D // LANES
    smem_blk = pl.BlockSpec((td,), lambda i, lo, hi, nu: (i,), memory_space=pltpu.SMEM)
    return pl.pallas_call(
        functools.partial(_dispatch_kernel, bm=bm),
        out_shape=jax.ShapeDtypeStruct((n_slots * slab, LANES), F32),
        grid_spec=pltpu.PrefetchScalarGridSpec(
            num_scalar_prefetch=3, grid=(T // td,),
            in_specs=[smem_blk, pl.BlockSpec((td, D), lambda i, lo, hi, nu: (i, 0))],
            out_specs=pl.BlockSpec(memory_space=pl.ANY),
            scratch_shapes=[pltpu.VMEM((2, td * slab, LANES), F32), pltpu.VMEM((bm * slab, LANES), F32),
                            pltpu.SemaphoreType.DMA((3,))]),
        compiler_params=_cparams(("arbitrary",)),
        name="moe_dispatch",
    )(pad_lo, pad_hi, n_used, pos, x2)


def _expert_kernel(ea_ref, eb_ref, nu_ref, xs_ref, wia_ref, woa_ref, wib_ref, wob_ref, ys_ref):
    del ea_ref, eb_ref
    used = pl.program_id(0) < nu_ref[0]

    @pl.when(used)
    def _():
        F = woa_ref.shape[1]
        slab = woa_ref.shape[2] // LANES
        bm = xs_ref.shape[0] // slab
        xb = _from_slab(xs_ref, (), bm, slab).astype(BF16)
        for half, (wi_ref, wo_ref) in enumerate(((wia_ref, woa_ref), (wib_ref, wob_ref))):
            h = _dot(xb, wi_ref[0])
            act = _silu(h[:, :F]) * h[:, F:]
            y = _dot(act.astype(BF16), wo_ref[0])
            for s in range(slab):
                ys_ref[pl.ds(half * slab + s, bm, stride=2 * slab), :] = y[:, s * LANES:(s + 1) * LANES]

    @pl.when(jnp.logical_not(used))
    def _():
        ys_ref[...] = jnp.zeros(ys_ref.shape, F32)


def _experts(block_a, block_b, n_used, xs, w_in, w_out, bm):
    E, D, F2 = w_in.shape
    slab = D // LANES
    n_blocks = xs.shape[0] // (bm * slab)
    row_blk = lambda i, ea, eb, nu: (jnp.maximum(jnp.minimum(i, nu[0] - 1), 0), 0)
    w_in_of = lambda table: pl.BlockSpec((1, D, F2), lambda i, ea, eb, nu: ((ea, eb)[table][i], 0, 0))
    w_out_of = lambda table: pl.BlockSpec((1, F2 // 2, D), lambda i, ea, eb, nu: ((ea, eb)[table][i], 0, 0))
    return pl.pallas_call(
        _expert_kernel,
        out_shape=jax.ShapeDtypeStruct((2 * xs.shape[0], LANES), F32),
        grid_spec=pltpu.PrefetchScalarGridSpec(
            num_scalar_prefetch=3, grid=(n_blocks,),
            in_specs=[pl.BlockSpec((bm * slab, LANES), row_blk), w_in_of(0), w_out_of(0), w_in_of(1), w_out_of(1)],
            out_specs=pl.BlockSpec((2 * bm * slab, LANES), lambda i, ea, eb, nu: (i, 0))),
        compiler_params=_cparams(("arbitrary",)),
        name="moe_experts",
    )(block_a, block_b, n_used, xs, w_in, w_out, w_in, w_out)


def _combine_kernel(pos_ref, nxt_ref, info_ref, x_ref, ys_ref, lng_ref, lnb_ref, o_ref, buf_ref, sem, *, alpha):
    tc, D = x_ref.shape
    slab2 = 2 * (D // LANES)
    i = pl.program_id(0)
    slot = i % 2

    def gather(p_ref, into):
        def issue(t, carry):
            src = ys_ref.at[pl.ds(pl.multiple_of(p_ref[t] * slab2, slab2), slab2)]
            dst = buf_ref.at[into, pl.ds(pl.multiple_of(t * slab2, slab2), slab2)]
            pltpu.make_async_copy(src, dst, sem.at[into]).start()
            return carry

        lax.fori_loop(0, tc, issue, 0, unroll=DMA_ISSUE_UNROLL)

    @pl.when(i == 0)
    def _():
        gather(pos_ref, slot)

    @pl.when(i + 1 < pl.num_programs(0))
    def _():
        gather(nxt_ref, 1 - slot)

    pltpu.make_async_copy(ys_ref.at[pl.ds(0, tc * slab2)], buf_ref.at[slot], sem.at[slot]).wait()
    info = info_ref[...]
    both = _from_slab(buf_ref, (slot,), tc, slab2)
    w1 = info[:, _R_W1:_R_W1 + 1]
    w2 = info[:, _R_W2:_R_W2 + 1]
    first_is_lower = info[:, _R_E1:_R_E1 + 1] < info[:, _R_E2:_R_E2 + 1]
    moe = jnp.where(first_is_lower, w1, w2) * both[:, :D] + jnp.where(first_is_lower, w2, w1) * both[:, D:]
    o_ref[...] = _layer_norm(alpha * x_ref[...] + moe, lng_ref[...], lnb_ref[...])


def _combine(pos, info, x2, ys, lng, lnb, alpha, tc):
    T, D = x2.shape
    n = T // tc
    cur = pl.BlockSpec((tc,), lambda i: (i,), memory_space=pltpu.SMEM)
    nxt = pl.BlockSpec((tc,), lambda i: (jnp.minimum(i + 1, n - 1),), memory_space=pltpu.SMEM)
    return pl.pallas_call(
        functools.partial(_combine_kernel, alpha=alpha),
        out_shape=jax.ShapeDtypeStruct((T, D), F32),
        grid=(n,),
        in_specs=[cur, nxt,
                  pl.BlockSpec((tc, LANES), lambda i: (i, 0)),
                  pl.BlockSpec((tc, D), lambda i: (i, 0)),
                  pl.BlockSpec(memory_space=pl.ANY),
                  pl.BlockSpec((1, D), lambda i: (0, 0)),
                  pl.BlockSpec((1, D), lambda i: (0, 0))],
        out_specs=pl.BlockSpec((tc, D), lambda i: (i, 0)),
        scratch_shapes=[pltpu.VMEM((2, 2 * tc * (D // LANES), LANES), F32), pltpu.SemaphoreType.DMA((2,))],
        compiler_params=_cparams(("arbitrary",)),
        name="moe_combine",
    )(pos, pos, info, x2, ys, lng, lnb)


def _moe(x2, info, rows, counts, w_e_in, w_e_out, lng, lnb, alpha, tiles):
    T, D = x2.shape
    bm = tiles["expert"]
    cnt = counts[0, :N_CLASSES].astype(I32)
    padded = ((cnt + bm - 1) // bm) * bm
    ends = jnp.cumsum(padded)
    offsets = ends - padded
    n_blocks = T // bm + N_CLASSES
    n_used = (ends[-1] // bm).astype(I32)
    blk_start = jnp.arange(n_blocks, dtype=I32) * bm
    block_class = jnp.sum((blk_start[:, None] >= ends[None, :]).astype(I32), axis=1)
    last_class = jnp.sum((blk_start[n_used - 1] >= ends).astype(I32))
    block_class = jnp.where(jnp.arange(n_blocks) < n_used, block_class, last_class)
    pairs = [(lo, hi) for lo in range(EXPERTS_PER_GROUP) for hi in range(lo + 1, EXPERTS_PER_GROUP)]
    groups = range(N_EXPERTS // EXPERTS_PER_GROUP)
    lower = jnp.asarray([g * EXPERTS_PER_GROUP + lo for g in groups for lo, _ in pairs], I32)
    higher = jnp.asarray([g * EXPERTS_PER_GROUP + hi for g in groups for _, hi in pairs], I32)

    pos = _positions(offsets.astype(I32), rows, tiles["pos"])[0]
    n_used = n_used.reshape(1)
    xs = _dispatch((offsets + cnt).astype(I32), ends.astype(I32), n_used, pos, x2, n_blocks * bm,
                   tiles["dispatch"], bm)
    ys = _experts(lower[block_class], higher[block_class], n_used, xs, w_e_in, w_e_out, bm)
    return _combine(pos, info, x2, ys, lng, lnb, alpha, tiles["combine"])


def _tiles(B, S):
    T = B * S
    pick = lambda want, n: math.gcd(want, n)
    return dict(mixer_batch=pick(MIXER_BATCH, B), mixer=pick(MIXER_ROWS, S), fox=pick(FOX_BLOCK, S), ret=pick(RET_ROWS, S), hgrn=pick(HGRN_ROWS, S),
                merge=pick(MERGE_ROWS, T),
                pos=pick(POS_LANES, T), dispatch=pick(DISPATCH_ROWS, T), expert=pick(EXPERT_ROWS, T),
                combine=pick(COMBINE_ROWS, T))


def _pack_input_weights(w_in, b_in):
    c = 3 * BRANCH_W
    n_mix = c + N_HEADS + 9 * BRANCH_W
    col_scale = jnp.ones((w_in.shape[-1],), F32).at[:BRANCH_W].set(HEAD_DIM ** -0.5 * LOG2E)
    wt = jnp.transpose(w_in * col_scale, (2, 0, 1))
    b = b_in * col_scale
    packed = []
    for l in range(w_in.shape[0]):
        w = wt[:, l, :]
        wt_mix = jnp.concatenate([w[:c + N_HEADS].astype(BF16), jnp.zeros((LANES - N_HEADS, w.shape[1]), BF16),
                                  w[c + N_HEADS:n_mix].astype(BF16)], axis=0)
        b_mix = jnp.concatenate([b[l, :c + N_HEADS], jnp.zeros((LANES - N_HEADS,), F32), b[l, c + N_HEADS:n_mix]])
        packed.append((wt_mix, b_mix[None, :], w[n_mix:].astype(BF16), b[l][None, n_mix:]))
    return packed


def kernel(x, mem, w_in, b_in, w_mem_kv, hgrn_lb, hgrn_norm_g, w_branch, w_out, ln_g, ln_b, w_router, b_router,
           w_e_in, w_e_out):
    B, S, D = x.shape
    T = B * S
    depth = w_in.shape[0]
    alpha = (2.0 * depth) ** 0.25
    tiles = _tiles(B, S)

    lb_all = jax.nn.softmax(hgrn_lb.astype(F32), axis=0)
    lb_all = jnp.cumsum(lb_all, axis=0) - lb_all[0:1]
    wr_pad = jnp.concatenate([w_router, jnp.zeros((D, LANES - N_EXPERTS), F32)], axis=1)
    br_pad = jnp.concatenate([b_router, jnp.full((LANES - N_EXPERTS,), MASK_VALUE, F32)])[None, :]

    x2 = x.reshape(T, D)
    packed = _pack_input_weights(w_in, b_in)
    for l in range(depth):
        wt_mix, b_mix, wt_gate, b_gate = packed[l]
        qkv, kb, o_ret, o_hg, o_mem = _mixer(x2.reshape(B, S, D), wt_mix, b_mix, mem, w_mem_kv[l].astype(BF16),
                                             lb_all[l][None, :], jnp.tile(hgrn_norm_g[l], N_HEADS)[None, :],
                                             tiles["mixer_batch"], tiles["mixer"], tiles["ret"], tiles["hgrn"])
        o_fox = _fox_attention(qkv, kb, tiles["fox"])
        branches = [o.reshape(T, BRANCH_W) for o in (o_fox, o_ret, o_hg, o_mem)]
        x2, info, rows, counts = _merge(x2, branches, wt_gate, b_gate, w_branch[l].astype(BF16),
                                        w_out[l].astype(BF16), ln_g[l, 0][None, :], ln_b[l, 0][None, :],
                                        wr_pad, br_pad, alpha, tiles["merge"])
        x2 = _moe(x2, info, rows, counts, w_e_in[l].astype(BF16), w_e_out[l].astype(BF16),
                  ln_g[l, 1][None, :], ln_b[l, 1][None, :], alpha, tiles)
    return x2.reshape(B, S, D)
```

```python
import functools
import math

import jax
import jax.numpy as jnp
from jax import lax
from jax.experimental import pallas as pl
from jax.experimental.pallas import tpu as pltpu

F32 = jnp.float32
BF16 = jnp.bfloat16
I32 = jnp.int32
HIGHEST = lax.Precision.HIGHEST

N_HEADS = 4
HEAD_DIM = 64
BRANCH_W = N_HEADS * HEAD_DIM
N_BRANCH = 4
RET_CHUNK = 64
ROPE_BASE = 10000.0
N_EXPERTS = 16
EXPERTS_PER_GROUP = 4
PAIRS_PER_GROUP = EXPERTS_PER_GROUP * (EXPERTS_PER_GROUP - 1) // 2
N_CLASSES = (N_EXPERTS // EXPERTS_PER_GROUP) * PAIRS_PER_GROUP
LN_EPS = 1e-5
GN_EPS = 1e-6
MASK_VALUE = -1e30
LOG2E = math.log2(math.e)

LANES = 128
V7X_VMEM_LIMIT_BYTES = 52 * 1024 * 1024

MIXER_ROWS = 512
MIXER_BATCH = 2
FOX_BLOCK = 512
RET_ROWS = 256
HGRN_ROWS = 128
HGRN_SUB = 4
MERGE_ROWS = 512
POS_LANES = 2048
DISPATCH_ROWS = 512
EXPERT_ROWS = 512
COMBINE_ROWS = 512
DMA_ISSUE_UNROLL = 8

_W_FOX = 3 * BRANCH_W
_C_FOXF = _W_FOX
_C_RET = _C_FOXF + LANES
_C_HGF = _C_RET + 4 * BRANCH_W
_C_HGR = _C_HGF + BRANCH_W
_C_MQ = _C_HGR + 3 * BRANCH_W
_C_END = _C_MQ + BRANCH_W

_NT = (((1,), (1,)), ((), ()))


def _cparams(sem):
    return pltpu.CompilerParams(dimension_semantics=sem, vmem_limit_bytes=V7X_VMEM_LIMIT_BYTES)


def _head_id(shape):
    return lax.broadcasted_iota(I32, shape, len(shape) - 1) // HEAD_DIM


def _dot(a, b):
    return jnp.dot(a, b, preferred_element_type=F32)


def _dot_nt(a, b):
    return lax.dot_general(a, b, _NT, preferred_element_type=F32)


def _silu(x):
    return x * jax.nn.sigmoid(x)


def _layer_norm(z, g, b):
    mu = jnp.mean(z, axis=-1, keepdims=True)
    d = z - mu
    var = jnp.mean(d * d, axis=-1, keepdims=True)
    return d * lax.rsqrt(var + LN_EPS) * g + b


_FOX_BIAS_PARTS = 3


def _fox_spare_lane(h):
    return HEAD_DIM * ((h + 1) % N_HEADS)


def _fox_place_table():
    lane = jnp.arange(BRANCH_W)
    head = jnp.arange(LANES)
    dest = jnp.where(head < N_HEADS, HEAD_DIM * ((head + 1) % N_HEADS), -BRANCH_W)
    return jnp.stack([(lane[None, :] == dest[:, None] + part) for part in range(_FOX_BIAS_PARTS)]).astype(BF16)


def _key_bias_block(ff, tri, place_ref, carry_ref):
    ls = jax.nn.log_sigmoid(ff) * LOG2E
    cs = jnp.dot(tri, ls, precision=HIGHEST, preferred_element_type=F32) + carry_ref[0:1, :]
    carry_ref[...] = jnp.broadcast_to(cs[ff.shape[0] - 1:, :], carry_ref.shape)
    yield
    rest = -cs
    kb = jnp.zeros((ff.shape[0], BRANCH_W), F32)
    for part in range(_FOX_BIAS_PARTS):
        piece = rest.astype(BF16)
        rest = rest - piece.astype(F32)
        kb = kb + _dot(piece, place_ref[part])
    return kb.astype(BF16)


def _fox_lane_table():
    lane = jnp.arange(BRANCH_W)
    rows = []
    for h in range(N_HEADS):
        spare = _fox_spare_lane(h)
        bias = (lane >= spare) & (lane < spare + _FOX_BIAS_PARTS)
        rows += [lane // HEAD_DIM == h, bias, ~bias, lane == spare]
    return jnp.stack(rows).astype(F32)


def _fox_kernel(q_ref, k_ref, v_ref, kb_ref, lanes_ref, o_ref, qh_ref, m_ref, acc_ref):
    qi = pl.program_id(1)
    ki = pl.program_id(2)
    blk = q_ref.shape[1]

    def pattern(h, r):
        return lanes_ref[4 * h + r:4 * h + r + 1, :].astype(BF16)

    @pl.when(ki == 0)
    def _():
        m_ref[...] = jnp.full(m_ref.shape, MASK_VALUE, F32)
        acc_ref[...] = jnp.zeros(acc_ref.shape, F32)
        q = q_ref[0]
        for h in range(N_HEADS):
            qh_ref[h] = q * pattern(h, 0) + pattern(h, 1)

    def sweep(masked):
        k = k_ref[0]
        v = v_ref[0]
        kb = kb_ref[0]
        if masked:
            causal = lax.broadcasted_iota(I32, (blk, blk), 0) >= lax.broadcasted_iota(I32, (blk, blk), 1)
        def logits(h):
            return _dot_nt(qh_ref[h], k * pattern(h, 2) + kb * pattern(h, 1))

        def softmax_step(h, s):
            if masked:
                s = jnp.where(causal, s, MASK_VALUE)
            m_prev = m_ref[h]
            m_new = jnp.maximum(m_prev, jnp.max(s, axis=-1, keepdims=True))
            m_ref[h] = m_new
            p = jnp.exp2(s - jnp.concatenate([m_new] * (blk // LANES), axis=1))
            return p.astype(BF16), jnp.exp2(m_prev - m_new)

        def accumulate(h, p, alpha):
            ones_lane = pattern(h, 3)
            vh = v * (1 - ones_lane) + ones_lane
            acc_ref[h] = acc_ref[h] * jnp.concatenate([alpha] * (BRANCH_W // LANES), axis=1) + _dot(p, vh)

        s_next = logits(0)
        for h in range(N_HEADS):
            s = s_next
            if h + 1 < N_HEADS:
                s_next = logits(h + 1)
            p, alpha = softmax_step(h, s)
            accumulate(h, p, alpha)

    @pl.when(ki < qi)
    def _():
        sweep(False)

    @pl.when(ki == qi)
    def _():
        sweep(True)
        lane = lax.broadcasted_iota(I32, (blk, BRANCH_W), 1)
        out = jnp.zeros((blk, BRANCH_W), F32)
        for h in range(N_HEADS):
            acc = acc_ref[h]
            spare = _fox_spare_lane(h)
            out = jnp.where((lane // HEAD_DIM) == h, acc / acc[:, spare:spare + 1], out)
        o_ref[0] = out.astype(BF16)


def _fox_attention(qkv3, kb, blk):
    B, S, _ = qkv3.shape
    n = S // blk
    kv_blk = lambda c: pl.BlockSpec((1, blk, BRANCH_W), lambda b, qi, ki: (b, jnp.minimum(ki, qi), c))
    lanes = _fox_lane_table()
    return pl.pallas_call(
        _fox_kernel,
        out_shape=jax.ShapeDtypeStruct((B, S, BRANCH_W), BF16),
        grid=(B, n, n),
        in_specs=[pl.BlockSpec((1, blk, BRANCH_W), lambda b, qi, ki: (b, qi, 0)), kv_blk(1), kv_blk(2),
                  pl.BlockSpec((1, blk, BRANCH_W), lambda b, qi, ki: (b, jnp.minimum(ki, qi), 0)),
                  pl.BlockSpec(lanes.shape, lambda b, qi, ki: (0, 0))],
        out_specs=pl.BlockSpec((1, blk, BRANCH_W), lambda b, qi, ki: (b, qi, 0)),
        scratch_shapes=[pltpu.VMEM((N_HEADS, blk, BRANCH_W), BF16),
                        pltpu.VMEM((N_HEADS, blk, LANES), F32),
                        pltpu.VMEM((N_HEADS, blk, BRANCH_W), F32)],
        compiler_params=_cparams(("parallel", "parallel", "arbitrary")),
        name="fox_attention",
    )(qkv3, qkv3, qkv3, kb, lanes)


def _mem_block(q, mk, mv):
    rows = q.shape[0]
    q = q * (HEAD_DIM ** -0.5)
    hid = _head_id((rows, BRANCH_W))
    out = jnp.zeros((rows, BRANCH_W), F32)
    for h in range(N_HEADS):
        qh = jnp.where(hid == h, q, jnp.zeros_like(q))
        s = _dot_nt(qh, mk)
        p = jnp.exp(s - jnp.max(s, axis=-1, keepdims=True))
        pv = _dot(p.astype(BF16), mv)
        out = jnp.where(hid == h, pv / jnp.sum(p, axis=-1, keepdims=True), out)
        yield
    return out.astype(BF16)


def _block_diag(val_per_head):
    hid = jnp.arange(BRANCH_W) // HEAD_DIM
    same = hid[:, None] == hid[None, :]
    return jnp.where(same, jnp.asarray(val_per_head, F32)[hid][:, None], 0.0)


def _rope_tables(S):
    half = HEAD_DIM // 2
    inv = ROPE_BASE ** (-2.0 * jnp.arange(half, dtype=F32) / HEAD_DIM)
    ang = jnp.arange(S, dtype=F32)[:, None] * inv[None, :]
    cos = jnp.tile(jnp.cos(ang), (1, 2 * N_HEADS))
    sin = jnp.tile(jnp.sin(ang), (1, 2 * N_HEADS))
    d = jnp.arange(BRANCH_W)
    lo = (d % HEAD_DIM) < half
    perm = jnp.where(lo[None, :] & (d[:, None] == d[None, :] + half), -1.0, 0.0)
    perm = perm + jnp.where((~lo)[None, :] & (d[:, None] == d[None, :] - half), 1.0, 0.0)
    return cos, sin, perm.astype(BF16)


def _ret_tables(R):
    log_gamma = jnp.log1p(-jnp.exp2(-5.0 - jnp.arange(N_HEADS, dtype=F32)))
    idx = jnp.arange(R, dtype=F32)
    chunk = jnp.arange(R) // RET_CHUNK
    dist = jnp.abs(idx[:, None] - idx[None, :])
    visible = chunk[None, :] <= chunk[:, None]
    decay = jnp.where(visible[None], jnp.exp(log_gamma[:, None, None] * dist[None]), 0.0)
    lanes_lg = jnp.repeat(log_gamma, HEAD_DIM)[None, :]
    qdec = jnp.exp(lanes_lg * (idx[:, None] + 1.0))
    kdec = jnp.exp(lanes_lg * (R - 1.0 - idx[:, None]))
    sdec = _block_diag(jnp.exp(log_gamma * R))
    return decay, qdec, kdec, sdec


def _ret_block(qb, kb, v, g, cos, sin, perm, decay_ref, qdec, kdec, sdec, bd, gmean, st_ref):
    R = qb.shape[0]
    qr = qb.astype(F32) * cos + _dot(qb, perm) * sin
    kr = (kb.astype(F32) * cos + _dot(kb, perm) * sin) * (HEAD_DIM ** -0.5)
    qrb = qr.astype(BF16)
    krb = kr.astype(BF16)
    hid = _head_id((R, BRANCH_W))
    yield

    st = st_ref[...]
    o = _dot((qr * qdec).astype(BF16), st.astype(BF16))
    for h in range(N_HEADS):
        qh = jnp.where(hid == h, qrb, jnp.zeros_like(qrb))
        a = _dot_nt(qh, krb) * decay_ref[h]
        o = o + jnp.where(hid == h, _dot(a.astype(BF16), v), 0.0)
        yield
    kd_t = (kr * kdec).T.astype(BF16)
    st_ref[...] = st * sdec + _dot(kd_t, v) * bd
    yield

    mu = _dot(o.astype(BF16), gmean)
    d = o - mu
    var = _dot((d * d).astype(BF16), gmean)
    on = d * lax.rsqrt(var + GN_EPS)
    return (on * _silu(g.astype(F32))).astype(BF16)


def _hgrn_levels(R):
    levels = []
    m = HGRN_SUB
    while 2 * m <= R:
        levels.append(m)
        m *= 2
    return tuple(levels)


def _hgrn_block(zf, vb, qb, g, lb, ng, tri, seg_ref, bd, gsum, gmean, st_ref):
    R = zf.shape[0]
    W = BRANCH_W
    kf = (1.0 - lb) * jax.nn.sigmoid(-zf)
    logf = jnp.log1p(-kf)
    b = jnp.dot(tri, logf, precision=HIGHEST, preferred_element_type=F32)
    yield
    q = qb.astype(F32)
    v = vb.astype(F32)
    hid = _head_id((R, W))
    row = lax.broadcasted_iota(I32, (R, W), 0)

    o = _dot((q * kf).astype(BF16), gsum) * v
    sub = row % HGRN_SUB
    for d in range(1, HGRN_SUB):
        e = jnp.exp(b - pltpu.roll(b, d, 0))
        w = jnp.where(sub >= d, q * pltpu.roll(kf, d, 0) * e, 0.0)
        o = o + _dot(w.astype(BF16), gsum) * pltpu.roll(v, d, 0)
        yield

    a_heads = [jnp.zeros((R, R), F32) for _ in range(N_HEADS)]
    for li, m in enumerate(_hgrn_levels(R)):
        nseg = R // (2 * m)
        bnd = jnp.broadcast_to(b.reshape(nseg, 2 * m, W)[:, m - 1:m, :], (nseg, 2 * m, W)).reshape(R, W)
        second = (row % (2 * m)) >= m
        qm = jnp.where(second, q * jnp.exp(b - bnd), 0.0).astype(BF16)
        km = jnp.where(second, 0.0, kf * jnp.exp(bnd - b)).astype(BF16)
        for h in range(N_HEADS):
            qh = jnp.where(hid == h, qm, jnp.zeros_like(qm))
            pairs = _dot_nt(qh, km)
            a_heads[h] = a_heads[h] + (pairs if nseg == 1 else pairs * seg_ref[li])
        yield
    for h in range(N_HEADS):
        o = o + jnp.where(hid == h, _dot(a_heads[h].astype(BF16), vb), 0.0)

    yield
    st = st_ref[...]
    o = o + _dot_nt((q * jnp.exp(b)).astype(BF16), st.astype(BF16))
    b_last = b[R - 1:R, :]
    kd = (kf * jnp.exp(b_last - b)).astype(BF16)
    st_ref[...] = st * jnp.exp(b_last) + _dot(v.T.astype(BF16), kd) * bd

    yield
    ms = _dot((o * o).astype(BF16), gmean)
    on = o * lax.rsqrt(ms + GN_EPS) * ng
    return (on * _silu(g.astype(F32))).astype(BF16)


def _run_interleaved(gens):
    results = [None] * len(gens)
    live = list(range(len(gens)))
    while live:
        for i in list(live):
            try:
                next(gens[i])
            except StopIteration as done:
                results[i] = done.value
                live.remove(i)
    return results


def _mixer_kernel(x_ref, wt_ref, b_ref, mem_ref, wkv_ref, cos_ref, sin_ref, perm_ref, decay_ref, qdec_ref,
                  kdec_ref, sdec_ref, bd_ref, gsum_ref, gmean_ref, lb_ref, ng_ref, tri_ref, seg_ref, place_ref,
                  qkv_ref, kb_ref, oret_ref, ohg_ref, omem_ref,
                  kv_ref, ret_st_ref, hg_st_ref, carry_ref):
    G, tm, D = x_ref.shape

    @pl.when(pl.program_id(1) == 0)
    def _():
        for g in range(G):
            kv_ref[g] = _dot(mem_ref[g].astype(BF16), wkv_ref[...]).astype(BF16)
        ret_st_ref[...] = jnp.zeros(ret_st_ref.shape, F32)
        hg_st_ref[...] = jnp.zeros(hg_st_ref.shape, F32)
        carry_ref[...] = jnp.zeros(carry_ref.shape, F32)

    xb = x_ref[...].reshape(G * tm, D).astype(BF16)
    W = BRANCH_W

    def seg(lo, hi):
        return _dot_nt(xb, wt_ref[lo:hi, :]) + b_ref[:, lo:hi]

    def tile_rows(g, r, n):
        return slice(g * tm + r * n, g * tm + (r + 1) * n), slice(r * n, (r + 1) * n)

    hgf = seg(_C_HGF, _C_HGR)
    hgr = seg(_C_HGR, _C_MQ).astype(BF16)
    ff = seg(_C_FOXF, _C_RET)
    ret = seg(_C_RET, _C_HGF).astype(BF16)
    mq = seg(_C_MQ, _C_END).astype(BF16)
    qkv = seg(0, _C_FOXF).astype(BF16)
    for g in range(G):
        qkv_ref[g] = qkv[g * tm:(g + 1) * tm]
    bd = bd_ref[...]
    gmean = gmean_ref[...]
    tri = tri_ref[...]
    R = tri.shape[0]
    RR = qdec_ref.shape[0]

    def hgrn_chain(g):
        for r in range(tm // R):
            src, dst = tile_rows(g, r, R)
            ohg_ref[g, dst, :] = yield from _hgrn_block(hgf[src], hgr[src, :W], hgr[src, W:2 * W], hgr[src, 2 * W:],
                                                        lb_ref[...], ng_ref[...], tri, seg_ref, bd, gsum_ref[...],
                                                        gmean, hg_st_ref.at[g])

    def ret_chain(g):
        for r in range(tm // RR):
            src, dst = tile_rows(g, r, RR)
            oret_ref[g, dst, :] = yield from _ret_block(ret[src, :W], ret[src, W:2 * W], ret[src, 2 * W:3 * W],
                                                        ret[src, 3 * W:], cos_ref[dst, :], sin_ref[dst, :],
                                                        perm_ref[...], decay_ref, qdec_ref[...], kdec_ref[...],
                                                        sdec_ref[...], bd, gmean, ret_st_ref.at[g])

    def key_bias_chain(g):
        for r in range(tm // R):
            src, dst = tile_rows(g, r, R)
            kb_ref[g, dst, :] = yield from _key_bias_block(ff[src], tri, place_ref, carry_ref.at[g])

    def mem_chain(g):
        omem_ref[g] = yield from _mem_block(mq[g * tm:(g + 1) * tm], kv_ref[g, :, :W], kv_ref[g, :, W:])

    _run_interleaved([chain(g) for chain in (hgrn_chain, ret_chain, key_bias_chain, mem_chain) for g in range(G)])


def _mixer(x3, wt_mix, b_mix, mem, w_kv, lb_row, ng_row, gb, tm, ret_rows, hgrn_rows):
    B, S, D = x3.shape
    M = mem.shape[1]
    cos, sin, perm = _rope_tables(S)
    decay, qdec, kdec, sdec = _ret_tables(ret_rows)
    bd = _block_diag(jnp.ones((N_HEADS,), F32))
    gsum = bd.astype(BF16)
    gmean = (bd / HEAD_DIM).astype(BF16)
    idx = jnp.arange(hgrn_rows)
    tri = (idx[:, None] >= idx[None, :]).astype(F32)
    seg = jnp.stack([(idx[:, None] // (2 * m) == idx[None, :] // (2 * m)).astype(F32)
                     for m in _hgrn_levels(hgrn_rows)])
    place = _fox_place_table()
    const = lambda a: pl.BlockSpec(a.shape, lambda b, j: (0,) * a.ndim)
    tile = lambda w: pl.BlockSpec((gb, tm, w), lambda b, j: (b, j, 0))
    pos_tab = pl.BlockSpec((tm, BRANCH_W), lambda b, j: (j, 0))
    return pl.pallas_call(
        _mixer_kernel,
        out_shape=[jax.ShapeDtypeStruct((B, S, _C_FOXF), BF16)] + [jax.ShapeDtypeStruct((B, S, BRANCH_W), BF16)] * 4,
        grid=(B // gb, S // tm),
        in_specs=[tile(D), const(wt_mix), const(b_mix),
                  pl.BlockSpec((gb, M, D), lambda b, j: (b, 0, 0)), const(w_kv),
                  pos_tab, pos_tab, const(perm), const(decay), const(qdec), const(kdec), const(sdec),
                  const(bd), const(gsum), const(gmean), const(lb_row), const(ng_row), const(tri), const(seg),
                  const(place)],
        out_specs=[tile(_C_FOXF)] + [tile(BRANCH_W)] * 4,
        scratch_shapes=[pltpu.VMEM((gb, M, 2 * BRANCH_W), BF16),
                        pltpu.VMEM((gb, BRANCH_W, BRANCH_W), F32),
                        pltpu.VMEM((gb, BRANCH_W, BRANCH_W), F32),
                        pltpu.VMEM((gb, 8, LANES), F32)],
        compiler_params=_cparams(("parallel", "arbitrary")),
        name="mixer",
    )(x3, wt_mix, b_mix, mem, w_kv, cos, sin, perm, decay, qdec, kdec, sdec, bd, gsum, gmean, lb_row, ng_row,
      tri, seg, place)


_R_W1, _R_W2, _R_E1, _R_E2, _R_CLASS, _R_RANK = range(6)


def _route_block(x1, wr_hi_ref, wr_lo_ref, br_ref, lstrict_ref, run_ref, counted):
    tm = x1.shape[0]
    x_hi = x1.astype(BF16)
    x_lo = (x1 - x_hi.astype(F32)).astype(BF16)
    logits = (_dot(x_hi, wr_hi_ref[...]) + _dot(x_hi, wr_lo_ref[...]) + _dot(x_lo, wr_hi_ref[...])) + br_ref[...]
    yield
    lane = lax.broadcasted_iota(I32, (tm, LANES), 1).astype(F32)
    group = jnp.floor(lane * (1.0 / EXPERTS_PER_GROUP))
    e = jnp.exp(logits - jnp.max(logits, axis=-1, keepdims=True))
    p = e / jnp.sum(e, axis=-1, keepdims=True)
    p1 = jnp.max(p, axis=-1, keepdims=True)
    e1 = jnp.min(jnp.where(p == p1, lane, float(LANES)), axis=-1, keepdims=True)
    in_group = group == jnp.floor(e1 * (1.0 / EXPERTS_PER_GROUP))
    rest = jnp.where(in_group, jnp.where(lane == e1, -1.0, p), -1.0)
    p2 = jnp.max(rest, axis=-1, keepdims=True)
    e2 = jnp.min(jnp.where(rest == p2, lane, float(LANES)), axis=-1, keepdims=True)
    w1 = p1 / (p1 + p2)
    w2 = p2 / (p1 + p2)
    yield

    grp = jnp.floor(e1 * (1.0 / EXPERTS_PER_GROUP))
    lo = jnp.minimum(e1, e2) - EXPERTS_PER_GROUP * grp
    hi = jnp.maximum(e1, e2) - EXPERTS_PER_GROUP * grp
    cls = grp * PAIRS_PER_GROUP + lo * (EXPERTS_PER_GROUP - 1) - lo * (lo - 1.0) * 0.5 + (hi - lo - 1.0)
    sel = jnp.where(lane == cls, 1.0, 0.0)
    before = _dot(lstrict_ref[...], sel.astype(BF16)) + run_ref[0:1, :]
    rank = jnp.sum(jnp.where(lane == cls, before, 0.0), axis=-1, keepdims=True)
    run_ref[...] = run_ref[...] + counted * jnp.sum(sel, axis=0, keepdims=True)
    yield

    info = jnp.zeros((tm, LANES), F32)
    for col, val in ((_R_W1, w1), (_R_W2, w2), (_R_E1, e1), (_R_E2, e2), (_R_CLASS, cls), (_R_RANK, rank)):
        info = jnp.where(lane == float(col), val, info)
    pick = (lax.broadcasted_iota(I32, (8, LANES), 0) == lax.broadcasted_iota(I32, (8, LANES), 1)).astype(F32)
    rows = lax.dot_general(pick, info, _NT, precision=HIGHEST, preferred_element_type=F32)
    return info, rows


def _merge_kernel(x_ref, b0_ref, b1_ref, b2_ref, b3_ref, wgt_ref, bg_ref, wb_ref, wo_ref, lng_ref, lnb_ref,
                  wr_hi_ref, wr_lo_ref, br_ref, lstrict_ref, o_ref, info_ref, rows_ref, cnt_ref, run_ref, x1_ref,
                  *, alpha):
    i = pl.program_id(0)

    @pl.when(i == 0)
    def _():
        run_ref[...] = jnp.zeros(run_ref.shape, F32)
        x1_ref[...] = jnp.zeros(x1_ref.shape, F32)

    x = x_ref[...]
    xb = x.astype(BF16)
    D = x.shape[1]

    def route_chain():
        info, rows = yield from _route_block(x1_ref[...], wr_hi_ref, wr_lo_ref, br_ref, lstrict_ref, run_ref,
                                             jnp.where(i > 0, 1.0, 0.0))
        info_ref[...] = info
        rows_ref[...] = rows
        cnt_ref[...] = run_ref[...]

    def merge_chain():
        mixed = jnp.zeros(x.shape, F32)
        for n, br in enumerate((b0_ref, b1_ref, b2_ref, b3_ref)):
            gate = jax.nn.sigmoid(_dot_nt(xb, wgt_ref[n * D:(n + 1) * D, :]) + bg_ref[:, n * D:(n + 1) * D])
            mixed = mixed + gate * _dot(br[...], wb_ref[n])
            yield
        return alpha * x + _dot(mixed.astype(BF16), wo_ref[...])

    _, z = _run_interleaved([route_chain(), merge_chain()])
    x1 = _layer_norm(z, lng_ref[...], lnb_ref[...])
    o_ref[...] = x1
    x1_ref[...] = x1


def _merge(x2, branches, wgt, bg, wb, wo, lng, lnb, wr_pad, br_pad, alpha, tm):
    T, D = x2.shape
    n = T // tm
    const = lambda a: pl.BlockSpec(a.shape, lambda i: (0,) * a.ndim)
    merged = lambda i: (jnp.minimum(i, n - 1), 0)
    routed = lambda i: jnp.maximum(i - 1, 0)
    br_spec = pl.BlockSpec((tm, BRANCH_W), merged)
    idx = jnp.arange(tm)
    lstrict = (idx[:, None] > idx[None, :]).astype(BF16)
    wr_hi = wr_pad.astype(BF16)
    wr_lo = (wr_pad - wr_hi.astype(F32)).astype(BF16)
    return pl.pallas_call(
        functools.partial(_merge_kernel, alpha=alpha),
        out_shape=[jax.ShapeDtypeStruct((T, D), F32), jax.ShapeDtypeStruct((T, LANES), F32),
                   jax.ShapeDtypeStruct((8, T), F32), jax.ShapeDtypeStruct((8, LANES), F32)],
        grid=(n + 1,),
        in_specs=[pl.BlockSpec((tm, D), merged), br_spec, br_spec, br_spec, br_spec,
                  const(wgt), const(bg), const(wb), const(wo), const(lng), const(lnb),
                  const(wr_hi), const(wr_lo), const(br_pad), const(lstrict)],
        out_specs=[pl.BlockSpec((tm, D), merged),
                   pl.BlockSpec((tm, LANES), lambda i: (routed(i), 0)),
                   pl.BlockSpec((8, tm), lambda i: (0, routed(i))),
                   pl.BlockSpec((8, LANES), lambda i: (0, 0))],
        scratch_shapes=[pltpu.VMEM((8, LANES), F32), pltpu.VMEM((tm, D), F32)],
        compiler_params=_cparams(("arbitrary",)),
        name="merge_ln_route",
    )(x2, *branches, wgt, bg, wb, wo, lng, lnb, wr_hi, wr_lo, br_pad, lstrict)


def _pos_kernel(off_ref, rows_ref, pos_ref):
    rows = rows_ref[...]
    cls = rows[_R_CLASS:_R_CLASS + 1, :]
    start = jnp.zeros(cls.shape, F32)
    for c in range(N_CLASSES):
        start = jnp.where(cls == float(c), off_ref[c].astype(F32), start)
    pos_ref[...] = jnp.zeros(pos_ref.shape, I32)
    pos_ref[0:1, :] = (rows[_R_RANK:_R_RANK + 1, :] + start).astype(I32)


def _positions(offsets, rows, tl):
    T = rows.shape[1]
    return pl.pallas_call(
        _pos_kernel,
        out_shape=jax.ShapeDtypeStruct((8, T), I32),
        grid_spec=pltpu.PrefetchScalarGridSpec(
            num_scalar_prefetch=1, grid=(T // tl,),
            in_specs=[pl.BlockSpec((8, tl), lambda i, off: (0, i))],
            out_specs=pl.BlockSpec((8, tl), lambda i, off: (0, i))),
        compiler_params=_cparams(("parallel",)),
        name="moe_positions",
    )(offsets, rows)


def _from_slab(ref, idx, rows, slab):
    return jnp.concatenate([ref[idx + (pl.ds(s, rows, stride=slab), slice(None))] for s in range(slab)], axis=1)


def _dispatch_kernel(pad_lo_ref, pad_hi_ref, nu_ref, pos_ref, x_ref, xs_ref, slab_ref, zero_ref, sem, *, bm):
    td, D = x_ref.shape
    slab = D // LANES
    i = pl.program_id(0)
    slot = i % 2

    def wait_slot(s):
        pltpu.make_async_copy(slab_ref.at[s], xs_ref.at[pl.ds(0, td * slab)], sem.at[s]).wait()

    @pl.when(i >= 2)
    def _():
        wait_slot(slot)

    x = x_ref[...]
    for s in range(slab):
        slab_ref[slot, pl.ds(s, td, stride=slab), :] = x[:, s * LANES:(s + 1) * LANES]

    def issue(t, carry):
        src = slab_ref.at[slot, pl.ds(pl.multiple_of(t * slab, slab), slab)]
        dst = xs_ref.at[pl.ds(pl.multiple_of(pos_ref[t] * slab, slab), slab)]
        pltpu.make_async_copy(src, dst, sem.at[slot]).start()
        return carry

    lax.fori_loop(0, td, issue, 0, unroll=DMA_ISSUE_UNROLL)

    last = pl.num_programs(0) - 1

    @pl.when(i == last)
    def _():
        wait_slot(slot)

    @pl.when(jnp.logical_and(i == last, i >= 1))
    def _():
        wait_slot(1 - slot)

    @pl.when(i == last)
    def _():
        zero_ref[...] = jnp.zeros(zero_ref.shape, F32)
        zero_row = zero_ref.at[pl.ds(0, slab)]
        n_blocks = xs_ref.shape[0] // (bm * slab)

        def row_copy(r):
            return pltpu.make_async_copy(zero_row, xs_ref.at[pl.ds(pl.multiple_of(r * slab, slab), slab)], sem.at[2])

        def block_copy(blk):
            start = pl.multiple_of(blk * (bm * slab), bm * slab)
            return pltpu.make_async_copy(zero_ref, xs_ref.at[pl.ds(start, bm * slab)], sem.at[2])

        def start_row(r, carry):
            row_copy(r).start()
            return carry

        def wait_row(r, carry):
            row_copy(r).wait()
            return carry

        def start_block(blk, carry):
            block_copy(blk).start()
            return carry

        def wait_block(blk, carry):
            block_copy(blk).wait()
            return carry

        for c in range(N_CLASSES):
            lax.fori_loop(pad_lo_ref[c], pad_hi_ref[c], start_row, 0)
        lax.fori_loop(nu_ref[0], n_blocks, start_block, 0)
        for c in range(N_CLASSES):
            lax.fori_loop(pad_lo_ref[c], pad_hi_ref[c], wait_row, 0)
        lax.fori_loop(nu_ref[0], n_blocks, wait_block, 0)


def _dispatch(pad_lo, pad_hi, n_used, pos, x2, n_slots, td, bm):
    T, D = x2.shape
    slab = D // LANES
    smem_blk = pl.BlockSpec((td,), lambda i, lo, hi, nu: (i,), memory_space=pltpu.SMEM)
    return pl.pallas_call(
        functools.partial(_dispatch_kernel, bm=bm),
        out_shape=jax.ShapeDtypeStruct((n_slots * slab, LANES), F32),
        grid_spec=pltpu.PrefetchScalarGridSpec(
            num_scalar_prefetch=3, grid=(T // td,),
            in_specs=[smem_blk, pl.BlockSpec((td, D), lambda i, lo, hi, nu: (i, 0))],
            out_specs=pl.BlockSpec(memory_space=pl.ANY),
            scratch_shapes=[pltpu.VMEM((2, td * slab, LANES), F32), pltpu.VMEM((bm * slab, LANES), F32),
                            pltpu.SemaphoreType.DMA((3,))]),
        compiler_params=_cparams(("arbitrary",)),
        name="moe_dispatch",
    )(pad_lo, pad_hi, n_used, pos, x2)


def _expert_kernel(ea_ref, eb_ref, nu_ref, xs_ref, wia_ref, woa_ref, wib_ref, wob_ref, ys_ref):
    del ea_ref, eb_ref
    used = pl.program_id(0) < nu_ref[0]

    @pl.when(used)
    def _():
        F = woa_ref.shape[1]
        slab = woa_ref.shape[2] // LANES
        bm = xs_ref.shape[0] // slab
        xb = _from_slab(xs_ref, (), bm, slab).astype(BF16)
        for half, (wi_ref, wo_ref) in enumerate(((wia_ref, woa_ref), (wib_ref, wob_ref))):
            h = _dot(xb, wi_ref[0])
            act = _silu(h[:, :F]) * h[:, F:]
            y = _dot(act.astype(BF16), wo_ref[0])
            for s in range(slab):
                ys_ref[pl.ds(half * slab + s, bm, stride=2 * slab), :] = y[:, s * LANES:(s + 1) * LANES]

    @pl.when(jnp.logical_not(used))
    def _():
        ys_ref[...] = jnp.zeros(ys_ref.shape, F32)


def _experts(block_a, block_b, n_used, xs, w_in, w_out, bm):
    E, D, F2 = w_in.shape
    slab = D // LANES
    n_blocks = xs.shape[0] // (bm * slab)
    row_blk = lambda i, ea, eb, nu: (jnp.maximum(jnp.minimum(i, nu[0] - 1), 0), 0)
    w_in_of = lambda table: pl.BlockSpec((1, D, F2), lambda i, ea, eb, nu: ((ea, eb)[table][i], 0, 0))
    w_out_of = lambda table: pl.BlockSpec((1, F2 // 2, D), lambda i, ea, eb, nu: ((ea, eb)[table][i], 0, 0))
    return pl.pallas_call(
        _expert_kernel,
        out_shape=jax.ShapeDtypeStruct((2 * xs.shape[0], LANES), F32),
        grid_spec=pltpu.PrefetchScalarGridSpec(
            num_scalar_prefetch=3, grid=(n_blocks,),
            in_specs=[pl.BlockSpec((bm * slab, LANES), row_blk), w_in_of(0), w_out_of(0), w_in_of(1), w_out_of(1)],
            out_specs=pl.BlockSpec((2 * bm * slab, LANES), lambda i, ea, eb, nu: (i, 0))),
        compiler_params=_cparams(("arbitrary",)),
        name="moe_experts",
    )(block_a, block_b, n_used, xs, w_in, w_out, w_in, w_out)


def _combine_kernel(pos_ref, nxt_ref, info_ref, x_ref, ys_ref, lng_ref, lnb_ref, o_ref, buf_ref, sem, *, alpha):
    tc, D = x_ref.shape
    slab2 = 2 * (D // LANES)
    i = pl.program_id(0)
    slot = i % 2

    def gather(p_ref, into):
        def issue(t, carry):
            src = ys_ref.at[pl.ds(pl.multiple_of(p_ref[t] * slab2, slab2), slab2)]
            dst = buf_ref.at[into, pl.ds(pl.multiple_of(t * slab2, slab2), slab2)]
            pltpu.make_async_copy(src, dst, sem.at[into]).start()
            return carry

        lax.fori_loop(0, tc, issue, 0, unroll=DMA_ISSUE_UNROLL)

    @pl.when(i == 0)
    def _():
        gather(pos_ref, slot)

    @pl.when(i + 1 < pl.num_programs(0))
    def _():
        gather(nxt_ref, 1 - slot)

    pltpu.make_async_copy(ys_ref.at[pl.ds(0, tc * slab2)], buf_ref.at[slot], sem.at[slot]).wait()
    info = info_ref[...]
    both = _from_slab(buf_ref, (slot,), tc, slab2)
    w1 = info[:, _R_W1:_R_W1 + 1]
    w2 = info[:, _R_W2:_R_W2 + 1]
    first_is_lower = info[:, _R_E1:_R_E1 + 1] < info[:, _R_E2:_R_E2 + 1]
    moe = jnp.where(first_is_lower, w1, w2) * both[:, :D] + jnp.where(first_is_lower, w2, w1) * both[:, D:]
    o_ref[...] = _layer_norm(alpha * x_ref[...] + moe, lng_ref[...], lnb_ref[...])


def _combine(pos, info, x2, ys, lng, lnb, alpha, tc):
    T, D = x2.shape
    n = T // tc
    cur = pl.BlockSpec((tc,), lambda i: (i,), memory_space=pltpu.SMEM)
    nxt = pl.BlockSpec((tc,), lambda i: (jnp.minimum(i + 1, n - 1),), memory_space=pltpu.SMEM)
    return pl.pallas_call(
        functools.partial(_combine_kernel, alpha=alpha),
        out_shape=jax.ShapeDtypeStruct((T, D), F32),
        grid=(n,),
        in_specs=[cur, nxt,
                  pl.BlockSpec((tc, LANES), lambda i: (i, 0)),
                  pl.BlockSpec((tc, D), lambda i: (i, 0)),
                  pl.BlockSpec(memory_space=pl.ANY),
                  pl.BlockSpec((1, D), lambda i: (0, 0)),
                  pl.BlockSpec((1, D), lambda i: (0, 0))],
        out_specs=pl.BlockSpec((tc, D), lambda i: (i, 0)),
        scratch_shapes=[pltpu.VMEM((2, 2 * tc * (D // LANES), LANES), F32), pltpu.SemaphoreType.DMA((2,))],
        compiler_params=_cparams(("arbitrary",)),
        name="moe_combine",
    )(pos, pos, info, x2, ys, lng, lnb)


def _moe(x2, info, rows, counts, w_e_in, w_e_out, lng, lnb, alpha, tiles):
    T, D = x2.shape
    bm = tiles["expert"]
    cnt = counts[0, :N_CLASSES].astype(I32)
    padded = ((cnt + bm - 1) // bm) * bm
    ends = jnp.cumsum(padded)
    offsets = ends - padded
    n_blocks = T // bm + N_CLASSES
    n_used = (ends[-1] // bm).astype(I32)
    blk_start = jnp.arange(n_blocks, dtype=I32) * bm
    block_class = jnp.sum((blk_start[:, None] >= ends[None, :]).astype(I32), axis=1)
    last_class = jnp.sum((blk_start[n_used - 1] >= ends).astype(I32))
    block_class = jnp.where(jnp.arange(n_blocks) < n_used, block_class, last_class)
    pairs = [(lo, hi) for lo in range(EXPERTS_PER_GROUP) for hi in range(lo + 1, EXPERTS_PER_GROUP)]
    groups = range(N_EXPERTS // EXPERTS_PER_GROUP)
    lower = jnp.asarray([g * EXPERTS_PER_GROUP + lo for g in groups for lo, _ in pairs], I32)
    higher = jnp.asarray([g * EXPERTS_PER_GROUP + hi for g in groups for _, hi in pairs], I32)

    pos = _positions(offsets.astype(I32), rows, tiles["pos"])[0]
    n_used = n_used.reshape(1)
    xs = _dispatch((offsets + cnt).astype(I32), ends.astype(I32), n_used, pos, x2, n_blocks * bm,
                   tiles["dispatch"], bm)
    ys = _experts(lower[block_class], higher[block_class], n_used, xs, w_e_in, w_e_out, bm)
    return _combine(pos, info, x2, ys, lng, lnb, alpha, tiles["combine"])


def _tiles(B, S):
    T = B * S
    pick = lambda want, n: math.gcd(want, n)
    return dict(mixer_batch=pick(MIXER_BATCH, B), mixer=pick(MIXER_ROWS, S), fox=pick(FOX_BLOCK, S), ret=pick(RET_ROWS, S), hgrn=pick(HGRN_ROWS, S),
                merge=pick(MERGE_ROWS, T),
                pos=pick(POS_LANES, T), dispatch=pick(DISPATCH_ROWS, T), expert=pick(EXPERT_ROWS, T),
                combine=pick(COMBINE_ROWS, T))


def _pack_input_weights(w_in, b_in):
    c = 3 * BRANCH_W
    n_mix = c + N_HEADS + 9 * BRANCH_W
    col_scale = jnp.ones((w_in.shape[-1],), F32).at[:BRANCH_W].set(HEAD_DIM ** -0.5 * LOG2E)
    wt = jnp.transpose(w_in * col_scale, (2, 0, 1))
    b = b_in * col_scale
    packed = []
    for l in range(w_in.shape[0]):
        w = wt[:, l, :]
        wt_mix = jnp.concatenate([w[:c + N_HEADS].astype(BF16), jnp.zeros((LANES - N_HEADS, w.shape[1]), BF16),
                                  w[c + N_HEADS:n_mix].astype(BF16)], axis=0)
        b_mix = jnp.concatenate([b[l, :c + N_HEADS], jnp.zeros((LANES - N_HEADS,), F32), b[l, c + N_HEADS:n_mix]])
        packed.append((wt_mix, b_mix[None, :], w[n_mix:].astype(BF16), b[l][None, n_mix:]))
    return packed


def kernel(x, mem, w_in, b_in, w_mem_kv, hgrn_lb, hgrn_norm_g, w_branch, w_out, ln_g, ln_b, w_router, b_router,
           w_e_in, w_e_out):
    B, S, D = x.shape
    T = B * S
    depth = w_in.shape[0]
    alpha = (2.0 * depth) ** 0.25
    tiles = _tiles(B, S)

    lb_all = jax.nn.softmax(hgrn_lb.astype(F32), axis=0)
    lb_all = jnp.cumsum(lb_all, axis=0) - lb_all[0:1]
    wr_pad = jnp.concatenate([w_router, jnp.zeros((D, LANES - N_EXPERTS), F32)], axis=1)
    br_pad = jnp.concatenate([b_router, jnp.full((LANES - N_EXPERTS,), MASK_VALUE, F32)])[None, :]

    x2 = x.reshape(T, D)
    packed = _pack_input_weights(w_in, b_in)
    for l in range(depth):
        wt_mix, b_mix, wt_gate, b_gate = packed[l]
        qkv, kb, o_ret, o_hg, o_mem = _mixer(x2.reshape(B, S, D), wt_mix, b_mix, mem, w_mem_kv[l].astype(BF16),
                                             lb_all[l][None, :], jnp.tile(hgrn_norm_g[l], N_HEADS)[None, :],
                                             tiles["mixer_batch"], tiles["mixer"], tiles["ret"], tiles["hgrn"])
        o_fox = _fox_attention(qkv, kb, tiles["fox"])
        branches = [o.reshape(T, BRANCH_W) for o in (o_fox, o_ret, o_hg, o_mem)]
        x2, info, rows, counts = _merge(x2, branches, wt_gate, b_gate, w_branch[l].astype(BF16),
                                        w_out[l].astype(BF16), ln_g[l, 0][None, :], ln_b[l, 0][None, :],
                                        wr_pad, br_pad, alpha, tiles["merge"])
        x2 = _moe(x2, info, rows, counts, w_e_in[l].astype(BF16), w_e_out[l].astype(BF16),
                  ln_g[l, 1][None, :], ln_b[l, 1][None, :], alpha, tiles)
    return x2.reshape(B, S, D)
```
